```python
import math
import jax, jax.numpy as jnp
from jax import lax
import numpy as np

D_MODEL = 1024
BATCH = 8
SEQ = 8192
DEPTH = 1
DEC_BATCH = 32
DEC_SEQ = 64
PAST_LEN = 1024

CHUNK = 64
SB_HEADS = 8
SB_HEAD_DIM = 128
SB_WIDTH = SB_HEADS * SB_HEAD_DIM
SB_BLOCK = 128
LRU_WIDTH = 1024
LRU_BLOCKS = 8
LRU_BLOCK_DIM = LRU_WIDTH // LRU_BLOCKS
CONV_WIDTH = 4
LRU_C = 8.0
MEM_TOKENS = 256
MEM_HEADS = 4
MEM_HEAD_DIM = 256
MEM_WIDTH = MEM_HEADS * MEM_HEAD_DIM
N_BRANCH = 3
FFN_HIDDEN = -(-8 * D_MODEL // (3 * 256)) * 256
IN_WIDTH = 3 * SB_WIDTH + 2 * LRU_WIDTH + MEM_WIDTH + N_BRANCH * D_MODEL
EPS = 1e-6

kernel_name = "stickbreak_rglru_memxattn_encoder_step"


def rms_norm(x, g):
    xf = x.astype(jnp.float32)
    y = xf * lax.rsqrt(jnp.mean(xf * xf, axis=-1, keepdims=True) + EPS)
    return (y * g.astype(jnp.float32)).astype(x.dtype)


def sb_block(q, k, v, q_pos, k_pos):
    z = jnp.einsum("bhqd,bhkd->bhqk", q.astype(jnp.float32), k.astype(jnp.float32)) / math.sqrt(SB_HEAD_DIM)
    mask = k_pos[None, :] < q_pos[:, None]
    log_beta = jax.nn.log_sigmoid(z)
    log_surv = jnp.where(mask, jax.nn.log_sigmoid(-z), 0.0)
    shifted = jnp.pad(log_surv[..., 1:], ((0, 0), (0, 0), (0, 0), (0, 1)))
    after = lax.cumsum(shifted, axis=3, reverse=True)
    a = jnp.where(mask, jnp.exp(log_beta + after), 0.0)
    return jnp.einsum("bhqk,bhkd->bhqd", a, v.astype(jnp.float32))


def stick_breaking_attention(q, k, v, q_pos, k_pos):
    b, h, tq, d = q.shape
    if tq > SB_BLOCK and tq % SB_BLOCK == 0:
        nb = tq // SB_BLOCK
        qb = q.reshape(b, h, nb, SB_BLOCK, d).transpose(2, 0, 1, 3, 4)
        pb = q_pos.reshape(nb, SB_BLOCK)
        ob = lax.map(lambda qp: sb_block(qp[0], k, v, qp[1], k_pos), (qb, pb))
        return ob.transpose(1, 2, 0, 3, 4).reshape(b, h, tq, v.shape[-1])
    return sb_block(q, k, v, q_pos, k_pos)


def causal_conv(x, buf, w, bias):
    t = x.shape[1]
    xp = jnp.concatenate([buf.astype(x.dtype), x], axis=1)
    y = bias
    for i in range(CONV_WIDTH):
        y = y + xp[:, i:i + t] * w[i]
    return y, xp[:, -(CONV_WIDTH - 1):]


def block_diag_linear(x, w, bias):
    xb = x.reshape(x.shape[:-1] + (LRU_BLOCKS, LRU_BLOCK_DIM))
    y = jnp.einsum("btnd,nde->btne", xb, w.astype(x.dtype))
    return y.reshape(x.shape) + bias.astype(x.dtype)


def rg_lru(x, h0, wa, ba, wx, bx, a_logit):
    xf = x.astype(jnp.float32)
    r = jax.nn.sigmoid(block_diag_linear(xf, wa, ba))
    i = jax.nn.sigmoid(block_diag_linear(xf, wx, bx))
    log_a = LRU_C * r * jax.nn.log_sigmoid(a_logit.astype(jnp.float32))
    a = jnp.exp(log_a)
    bterm = jnp.sqrt(-jnp.expm1(2.0 * log_a)) * (i * xf)
    bterm = bterm.at[:, 0].add(a[:, 0] * h0.astype(jnp.float32))

    def combine(c1, c2):
        a1, b1 = c1
        a2, b2 = c2
        return a1 * a2, a2 * b1 + b2

    _, h = lax.associative_scan(combine, (a, bterm), axis=1)
    return h, h[:, -1]


def memory_kv(mem, g, wk, wv, k_norm_g):
    b, n, _ = mem.shape
    m = rms_norm(mem, g)
    k = rms_norm((m @ wk).reshape(b, n, MEM_HEADS, MEM_HEAD_DIM), k_norm_g)
    v = (m @ wv).reshape(b, n, MEM_HEADS, MEM_HEAD_DIM)
    return k.transpose(0, 2, 1, 3), v.transpose(0, 2, 1, 3)


def layer(x, past_k, past_v, conv_buf, h0, mem_k, mem_v, lw):
    b, t, _ = x.shape
    p = past_k.shape[2]
    xn = rms_norm(x, lw["norm_mix_g"])
    proj = xn @ lw["w_in"]
    splits = [SB_WIDTH, 2 * SB_WIDTH, 3 * SB_WIDTH, 3 * SB_WIDTH + LRU_WIDTH,
              3 * SB_WIDTH + 2 * LRU_WIDTH, 3 * SB_WIDTH + 2 * LRU_WIDTH + MEM_WIDTH]
    q_sb, k_sb, v_sb, x_lru, g_lru, q_mem, gates = jnp.split(proj, splits, axis=-1)

    heads = lambda z: z.reshape(b, t, SB_HEADS, SB_HEAD_DIM).transpose(0, 2, 1, 3)
    q_sb, k_new, v_new = heads(q_sb), heads(k_sb), heads(v_sb)
    k_all = jnp.concatenate([past_k.astype(k_new.dtype), k_new], axis=2)
    v_all = jnp.concatenate([past_v.astype(v_new.dtype), v_new], axis=2)
    q_pos = p + jnp.arange(t, dtype=jnp.int32)
    k_pos = jnp.arange(p + t, dtype=jnp.int32)
    o_sb = stick_breaking_attention(q_sb, k_all, v_all, q_pos, k_pos)
    o_sb = o_sb.transpose(0, 2, 1, 3).reshape(b, t, SB_WIDTH).astype(x.dtype)

    xc, new_buf = causal_conv(x_lru, conv_buf, lw["conv_w"], lw["conv_b"])
    h, h_last = rg_lru(xc, h0, lw["lru_wa"], lw["lru_ba"], lw["lru_wx"], lw["lru_bx"], lw["lru_a_logit"])
    o_lru = (h * jax.nn.gelu(g_lru.astype(jnp.float32), approximate=True)).astype(x.dtype)

    qm = rms_norm(q_mem.reshape(b, t, MEM_HEADS, MEM_HEAD_DIM), lw["q_norm_g"])
    s = jnp.einsum("bthd,bhnd->bhtn", qm.astype(jnp.float32), mem_k.astype(jnp.float32)) / math.sqrt(MEM_HEAD_DIM)
    pm = jax.nn.softmax(s, axis=-1)
    o_mem = jnp.einsum("bhtn,bhnd->bthd", pm, mem_v.astype(jnp.float32)).reshape(b, t, MEM_WIDTH).astype(x.dtype)

    g = jax.nn.sigmoid(gates.astype(jnp.float32) + lw["b_merge"].astype(jnp.float32)).reshape(b, t, N_BRANCH, D_MODEL)
    merged = (g[:, :, 0] * (o_sb @ lw["w_br_sb"]) + g[:, :, 1] * (o_lru @ lw["w_br_lru"])
              + g[:, :, 2] * (o_mem @ lw["w_br_mem"])).astype(x.dtype)
    x = x + merged @ lw["w_out"]

    xn2 = rms_norm(x, lw["norm_ffn_g"])
    f = (jax.nn.silu(xn2 @ lw["w_ffn_gate"]) * (xn2 @ lw["w_ffn_up"])) @ lw["w_ffn_down"]
    x = x + f
    return x, k_new, v_new, new_buf, h_last.astype(x.dtype)


def setup_inputs(seed: int = 0) -> dict:
    key = jax.random.key(seed)
    ks = jax.random.split(key, 32)
    f32 = jnp.float32

    def nrm(k, shape, scale=1.0):
        return jax.random.normal(k, shape, f32) * scale

    u = jax.random.uniform(ks[18], (DEPTH, LRU_WIDTH), f32, 0.9, 0.999)
    a_base = u ** (1.0 / LRU_C)
    lru_a_logit = jnp.log(a_base) - jnp.log1p(-a_base)
    return {
        "x_prompt": nrm(ks[0], (BATCH, SEQ, D_MODEL)),
        "x_sample": nrm(ks[1], (DEC_BATCH, DEC_SEQ, D_MODEL)),
        "mem_prompt": nrm(ks[2], (BATCH, MEM_TOKENS, D_MODEL)),
        "cache_sb_k": nrm(ks[3], (DEPTH, DEC_BATCH, SB_HEADS, PAST_LEN, SB_HEAD_DIM)),
        "cache_sb_v": nrm(ks[4], (DEPTH, DEC_BATCH, SB_HEADS, PAST_LEN, SB_HEAD_DIM)),
        "state_conv": nrm(ks[5], (DEPTH, DEC_BATCH, CONV_WIDTH - 1, LRU_WIDTH)),
        "state_lru_h": nrm(ks[6], (DEPTH, DEC_BATCH, LRU_WIDTH), 0.5),
        "cache_mem_k": nrm(ks[7], (DEPTH, DEC_BATCH, MEM_HEADS, MEM_TOKENS, MEM_HEAD_DIM)),
        "cache_mem_v": nrm(ks[8], (DEPTH, DEC_BATCH, MEM_HEADS, MEM_TOKENS, MEM_HEAD_DIM)),
        "norm_mix_g": 1.0 + nrm(ks[9], (DEPTH, D_MODEL), 0.01),
        "w_in": nrm(ks[10], (DEPTH, D_MODEL, IN_WIDTH), D_MODEL ** -0.5),
        "b_merge": nrm(ks[11], (DEPTH, N_BRANCH * D_MODEL), 0.01),
        "conv_w": nrm(ks[12], (DEPTH, CONV_WIDTH, LRU_WIDTH), CONV_WIDTH ** -0.5),
        "conv_b": nrm(ks[13], (DEPTH, LRU_WIDTH), 0.01),
        "lru_wa": nrm(ks[14], (DEPTH, LRU_BLOCKS, LRU_BLOCK_DIM, LRU_BLOCK_DIM), LRU_BLOCK_DIM ** -0.5),
        "lru_ba": nrm(ks[15], (DEPTH, LRU_WIDTH), 0.01),
        "lru_wx": nrm(ks[16], (DEPTH, LRU_BLOCKS, LRU_BLOCK_DIM, LRU_BLOCK_DIM), LRU_BLOCK_DIM ** -0.5),
        "lru_bx": nrm(ks[17], (DEPTH, LRU_WIDTH), 0.01),
        "lru_a_logit": lru_a_logit,
        "q_norm_g": 1.0 + nrm(ks[19], (DEPTH, MEM_HEAD_DIM), 0.01),
        "k_norm_g": 1.0 + nrm(ks[20], (DEPTH, MEM_HEAD_DIM), 0.01),
        "mem_norm_g": 1.0 + nrm(ks[21], (DEPTH, D_MODEL), 0.01),
        "w_mem_k": nrm(ks[22], (DEPTH, D_MODEL, MEM_WIDTH), D_MODEL ** -0.5),
        "w_mem_v": nrm(ks[23], (DEPTH, D_MODEL, MEM_WIDTH), D_MODEL ** -0.5),
        "w_br_sb": nrm(ks[24], (DEPTH, SB_WIDTH, D_MODEL), SB_WIDTH ** -0.5),
        "w_br_lru": nrm(ks[25], (DEPTH, LRU_WIDTH, D_MODEL), LRU_WIDTH ** -0.5),
        "w_br_mem": nrm(ks[26], (DEPTH, MEM_WIDTH, D_MODEL), MEM_WIDTH ** -0.5),
        "w_out": nrm(ks[27], (DEPTH, D_MODEL, D_MODEL), D_MODEL ** -0.5),
        "norm_ffn_g": 1.0 + nrm(ks[28], (DEPTH, D_MODEL), 0.01),
        "w_ffn_gate": nrm(ks[29], (DEPTH, D_MODEL, FFN_HIDDEN), D_MODEL ** -0.5),
        "w_ffn_up": nrm(ks[30], (DEPTH, D_MODEL, FFN_HIDDEN), D_MODEL ** -0.5),
        "w_ffn_down": nrm(ks[31], (DEPTH, FFN_HIDDEN, D_MODEL), FFN_HIDDEN ** -0.5),
    }


def reference(x_prompt, x_sample, mem_prompt, cache_sb_k, cache_sb_v, state_conv, state_lru_h,
              cache_mem_k, cache_mem_v, norm_mix_g, w_in, b_merge, conv_w, conv_b, lru_wa, lru_ba,
              lru_wx, lru_bx, lru_a_logit, q_norm_g, k_norm_g, mem_norm_g, w_mem_k, w_mem_v,
              w_br_sb, w_br_lru, w_br_mem, w_out, norm_ffn_g, w_ffn_gate, w_ffn_up, w_ffn_down):
    hp, hs = x_prompt, x_sample
    bp = x_prompt.shape[0]
    kp_l, vp_l, cp_l, sp_l, mkp_l, mvp_l = [], [], [], [], [], []
    ks_l, vs_l, cs_l, ss_l = [], [], [], []
    for l in range(DEPTH):
        lw = {
            "norm_mix_g": norm_mix_g[l], "w_in": w_in[l], "b_merge": b_merge[l],
            "conv_w": conv_w[l], "conv_b": conv_b[l], "lru_wa": lru_wa[l], "lru_ba": lru_ba[l],
            "lru_wx": lru_wx[l], "lru_bx": lru_bx[l], "lru_a_logit": lru_a_logit[l],
            "q_norm_g": q_norm_g[l], "w_br_sb": w_br_sb[l], "w_br_lru": w_br_lru[l],
            "w_br_mem": w_br_mem[l], "w_out": w_out[l], "norm_ffn_g": norm_ffn_g[l],
            "w_ffn_gate": w_ffn_gate[l], "w_ffn_up": w_ffn_up[l], "w_ffn_down": w_ffn_down[l],
        }
        mk_p, mv_p = memory_kv(mem_prompt, mem_norm_g[l], w_mem_k[l], w_mem_v[l], k_norm_g[l])
        empty = jnp.zeros((bp, SB_HEADS, 0, SB_HEAD_DIM), hp.dtype)
        hp, k_p, v_p, c_p, s_p = layer(hp, empty, empty,
                                       jnp.zeros((bp, CONV_WIDTH - 1, LRU_WIDTH), hp.dtype),
                                       jnp.zeros((bp, LRU_WIDTH), hp.dtype), mk_p, mv_p, lw)
        kp_l.append(k_p); vp_l.append(v_p); cp_l.append(c_p); sp_l.append(s_p)
        mkp_l.append(mk_p); mvp_l.append(mv_p)
        hs, k_s, v_s, c_s, s_s = layer(hs, cache_sb_k[l], cache_sb_v[l], state_conv[l], state_lru_h[l],
                                       cache_mem_k[l], cache_mem_v[l], lw)
        ks_l.append(k_s); vs_l.append(v_s); cs_l.append(c_s); ss_l.append(s_s)
    return (hp, hs,
            jnp.stack(kp_l), jnp.stack(vp_l), jnp.stack(cp_l), jnp.stack(sp_l),
            jnp.stack(mkp_l), jnp.stack(mvp_l),
            jnp.stack(ks_l), jnp.stack(vs_l), jnp.stack(cs_l), jnp.stack(ss_l))
```

```python
import functools
import math

import jax
import jax.numpy as jnp
from jax import lax
from jax.experimental import pallas as pl
from jax.experimental.pallas import tpu as pltpu

F32 = jnp.float32
BF16 = jnp.bfloat16

EPS = 1e-6
SB_HEADS = 8
SB_HEAD_DIM = 128
LRU_BLOCKS = 8
LRU_C = 8.0
CONV_WIDTH = 4
MEM_HEADS = 4
N_BRANCH = 3

V7X_SUBLANES = 8
V7X_MXU_DIM = 256
V7X_VMEM_LIMIT_BYTES = 56 * 1024 * 1024

SB_TILE = V7X_MXU_DIM
TOKEN_TILE = 512
LRU_TILE = 256


def _dot(a, b):
    return jnp.dot(a, b, preferred_element_type=F32)


def _dot_nt(a, b):
    return lax.dot_general(a, b, (((1,), (1,)), ((), ())), preferred_element_type=F32)


def _sigmoid(x):
    return 1.0 / (1.0 + jnp.exp(-x))


def _rms(x, g):
    ms = jnp.mean(x * x, axis=-1, keepdims=True)
    return x * lax.rsqrt(ms + EPS) * g


def _const_spec(shape):
    zeros = (0,) * len(shape)
    return pl.BlockSpec(shape, lambda *_: zeros, pipeline_mode=pl.Buffered(1))


def _params(semantics):
    return pltpu.CompilerParams(dimension_semantics=semantics, vmem_limit_bytes=V7X_VMEM_LIMIT_BYTES)


def _qkv_kernel(x_ref, g_ref, w_ref, xn_ref, q_ref, kb_ref, vb_ref, kf_ref, vf_ref):
    xn = _rms(x_ref[...], g_ref[...]).astype(BF16)
    xn_ref[...] = xn
    bb, nh, tt, hd = q_ref.shape
    width = nh * hd
    q_scale = 1.0 / math.sqrt(hd)
    for part, (b_ref, f_ref) in enumerate(((q_ref, None), (kb_ref, kf_ref), (vb_ref, vf_ref))):
        res = _dot(xn, w_ref[:, part * width:(part + 1) * width])
        for h in range(nh):
            blk = res[:, h * hd:(h + 1) * hd].reshape(bb, tt, hd)
            if f_ref is None:
                blk = blk * q_scale
            else:
                f_ref[:, h] = blk
            b_ref[:, h] = blk.astype(BF16)


def _qkv_proj(x2, g, w_qkv, batch, seq):
    m, d = x2.shape
    nh, hd = SB_HEADS, SB_HEAD_DIM
    tm = min(TOKEN_TILE, m)
    if seq >= tm:
        bb, tt = 1, tm
    else:
        bb, tt = tm // seq, seq
    nt = seq // tt
    head_map = lambda i: (i // nt, 0, i % nt, 0)
    head_spec = pl.BlockSpec((bb, nh, tt, hd), head_map)
    head_bf = jax.ShapeDtypeStruct((batch, nh, seq, hd), BF16)
    head_f32 = jax.ShapeDtypeStruct((batch, nh, seq, hd), F32)
    return pl.pallas_call(
        _qkv_kernel,
        grid=(m // tm,),
        in_specs=[pl.BlockSpec((tm, d), lambda i: (i, 0)), _const_spec((1, d)), _const_spec(w_qkv.shape)],
        out_specs=[pl.BlockSpec((tm, d), lambda i: (i, 0)), head_spec, head_spec, head_spec, head_spec,
                   head_spec],
        out_shape=[jax.ShapeDtypeStruct((m, d), BF16), head_bf, head_bf, head_bf, head_f32, head_f32],
        compiler_params=_params(("parallel",)),
        name="qkv_proj",
    )(x2, g, w_qkv)


def _sb_step(q, k, v, u, c, acc, mask):
    z = _dot_nt(q, k)
    softplus = jnp.maximum(z, 0.0) + jnp.log(1.0 + jnp.exp(-jnp.abs(z)))
    log_beta = z - softplus
    log_surv = -softplus if mask is None else jnp.where(mask, -softplus, 0.0)
    hi = log_surv.astype(BF16)
    lo = (log_surv - hi.astype(F32)).astype(BF16)
    after = _dot(hi, u) + _dot(lo, u)
    w = jnp.exp(log_beta + after + c)
    if mask is not None:
        w = jnp.where(mask, w, 0.0)
    acc = acc + _dot(w.astype(BF16), v)
    c = c + jnp.sum(log_surv, axis=-1, keepdims=True)
    return c, acc


def _causal_mask(tq, tk):
    row = lax.broadcasted_iota(jnp.int32, (tq, tk), 0)
    col = lax.broadcasted_iota(jnp.int32, (tq, tk), 1)
    return col < row


def _sb_prompt_kernel(q_ref, k_ref, v_ref, u_ref, o_ref):
    tq, hd = q_ref.shape
    tk = tq
    qi = pl.program_id(2)
    q = q_ref[...]
    u = u_ref[...]

    def block(kb):
        start = pl.multiple_of(kb * tk, tk)
        return k_ref[pl.ds(start, tk), :], v_ref[pl.ds(start, tk), :]

    k, v = block(qi)
    c, acc = _sb_step(q, k, v, u, jnp.zeros((tq, 1), F32), jnp.zeros((tq, hd), F32), _causal_mask(tq, tk))

    def body(i, carry):
        k, v = block(qi - 1 - i)
        return _sb_step(q, k, v, u, carry[0], carry[1], None)

    c, acc = lax.fori_loop(0, qi, body, (c, acc))
    o_ref[...] = acc.astype(o_ref.dtype)


def _sb_prompt(q, k, v, u):
    b, nh, t, hd = q.shape
    tq = min(SB_TILE, t)
    nq = t // tq
    kv_spec = pl.BlockSpec((None, None, t, hd), lambda bi, h, qi: (bi, h, 0, 0))
    return pl.pallas_call(
        _sb_prompt_kernel,
        grid=(b, nh, nq),
        in_specs=[pl.BlockSpec((None, None, tq, hd), lambda bi, h, qi: (bi, h, qi, 0)), kv_spec, kv_spec,
                  _const_spec(u.shape)],
        out_specs=pl.BlockSpec((tq, hd), lambda bi, h, qi: (bi * nq + qi, h)),
        out_shape=jax.ShapeDtypeStruct((b * t, nh * hd), BF16),
        compiler_params=_params(("parallel", "parallel", "arbitrary")),
        name="sb_prompt",
    )(q, k, v, u)


def _sb_sample_kernel(q_ref, kn_ref, vn_ref, kp_ref, vp_ref, u_ref, us_ref, o_ref):
    nh, t, hd = q_ref.shape
    past = kp_ref.shape[1]
    tk = u_ref.shape[0]
    u = u_ref[...]
    us = us_ref[...]
    mask = _causal_mask(t, t)
    for h in range(nh):
        q = q_ref[h]
        c, acc = _sb_step(q, kn_ref[h], vn_ref[h], us, jnp.zeros((t, 1), F32), jnp.zeros((t, hd), F32), mask)

        def body(i, carry, h=h, q=q):
            start = pl.multiple_of(past - (i + 1) * tk, tk)
            k = kp_ref[h, pl.ds(start, tk), :].astype(BF16)
            v = vp_ref[h, pl.ds(start, tk), :].astype(BF16)
            return _sb_step(q, k, v, u, carry[0], carry[1], None)

        c, acc = lax.fori_loop(0, past // tk, body, (c, acc))
        o_ref[:, h * hd:(h + 1) * hd] = acc.astype(o_ref.dtype)


def _sb_sample(q, k_new, v_new, k_past, v_past, u, u_small):
    b, nh, t, hd = q.shape
    past = k_past.shape[2]
    new_spec = pl.BlockSpec((None, nh, t, hd), lambda bi: (bi, 0, 0, 0))
    past_spec = pl.BlockSpec((None, nh, past, hd), lambda bi: (bi, 0, 0, 0))
    return pl.pallas_call(
        _sb_sample_kernel,
        grid=(b,),
        in_specs=[new_spec, new_spec, new_spec, past_spec, past_spec, _const_spec(u.shape),
                  _const_spec(u_small.shape)],
        out_specs=pl.BlockSpec((t, nh * hd), lambda bi: (bi, 0)),
        out_shape=jax.ShapeDtypeStruct((b * t, nh * hd), BF16),
        compiler_params=_params(("parallel",)),
        name="sb_sample",
    )(q, k_new, v_new, k_past, v_past, u, u_small)


def _strict_upper(n):
    r = lax.broadcasted_iota(jnp.int32, (n, n), 0)
    c = lax.broadcasted_iota(jnp.int32, (n, n), 1)
    return (r > c).astype(BF16)


def _lru_kernel(xn_ref, w_ref, cw_ref, cb_ref, wa_ref, wx_ref, ba_ref, bx_ref, al_ref, cs_ref, h0_ref,
                o_ref, nc_ref, hl_ref, xbuf, hcar):
    tt, width = o_ref.shape
    tail = CONV_WIDTH - 1
    t = pl.program_id(1)
    last = pl.num_programs(1) - 1

    @pl.when(t == 0)
    def _():
        xbuf[0:V7X_SUBLANES, :] = jnp.zeros((V7X_SUBLANES, width), F32)
        xbuf[V7X_SUBLANES - tail:V7X_SUBLANES, :] = cs_ref[...]
        hcar[...] = h0_ref[...]

    xn = xn_ref[...]
    xl = _dot(xn, w_ref[:, :width])
    gl = _dot(xn, w_ref[:, width:])

    xbuf[V7X_SUBLANES:V7X_SUBLANES + tt, :] = xl
    xc = cb_ref[...]
    for i in range(tail):
        xc = xc + xbuf[V7X_SUBLANES - tail + i:V7X_SUBLANES - tail + i + tt, :] * cw_ref[i:i + 1, :]
    xc = xc + xl * cw_ref[tail:tail + 1, :]
    xbuf[0:V7X_SUBLANES, :] = xbuf[tt:tt + V7X_SUBLANES, :]

    @pl.when(t == last)
    def _():
        nc_ref[...] = xbuf[V7X_SUBLANES - tail:V7X_SUBLANES, :]

    nb = wa_ref.shape[0]
    bd = width // nb
    ra, rx = [], []
    for n in range(nb):
        xcn = xc[:, n * bd:(n + 1) * bd].astype(BF16)
        ra.append(_dot(xcn, wa_ref[n]))
        rx.append(_dot(xcn, wx_ref[n]))
    r = _sigmoid(jnp.concatenate(ra, axis=1) + ba_ref[...])
    ig = _sigmoid(jnp.concatenate(rx, axis=1) + bx_ref[...])

    al = al_ref[...]
    log_sig = jnp.minimum(al, 0.0) - jnp.log(1.0 + jnp.exp(-jnp.abs(al)))
    log_a = LRU_C * r * log_sig
    a = jnp.exp(log_a)
    th = jnp.tanh(log_a)
    bterm = jnp.sqrt(-2.0 * th / (1.0 - th)) * (ig * xc)

    groups = tt // V7X_SUBLANES
    a3 = a.reshape(groups, V7X_SUBLANES, width)
    b3 = bterm.reshape(groups, V7X_SUBLANES, width)
    sub = lax.broadcasted_iota(jnp.int32, a3.shape, 1)
    shift = 1
    while shift < V7X_SUBLANES:
        valid = sub >= shift
        b3 = jnp.where(valid, a3 * pltpu.roll(b3, shift, axis=1) + b3, b3)
        a3 = jnp.where(valid, a3 * pltpu.roll(a3, shift, axis=1), a3)
        shift *= 2
    h = hcar[...]
    rows = []
    for gi in range(groups):
        hg = b3[gi] + a3[gi] * h
        rows.append(hg)
        h = hg[V7X_SUBLANES - 1:V7X_SUBLANES, :]
    hcar[...] = h

    @pl.when(t == last)
    def _():
        hl_ref[...] = h

    hs = jnp.concatenate(rows, axis=0)
    gelu = 0.5 * gl * (1.0 + jnp.tanh(0.7978845608028654 * (gl + 0.044715 * (gl * gl * gl))))
    o_ref[...] = (hs * gelu).astype(o_ref.dtype)


def _lru_branch(xn, w_lru, conv_w, conv_b, wa, wx, ba, bx, a_logit, conv_state, h0, batch, seq):
    m, d = xn.shape
    width = w_lru.shape[1] // 2
    tail = CONV_WIDTH - 1
    tt = min(LRU_TILE, seq)
    nt = seq // tt
    return pl.pallas_call(
        _lru_kernel,
        grid=(batch, nt),
        in_specs=[pl.BlockSpec((tt, d), lambda b, t: (b * nt + t, 0)), _const_spec(w_lru.shape),
                  _const_spec(conv_w.shape), _const_spec((1, width)), _const_spec(wa.shape),
                  _const_spec(wx.shape), _const_spec((1, width)), _const_spec((1, width)),
                  _const_spec((1, width)),
                  pl.BlockSpec((None, tail, width), lambda b, t: (b, 0, 0)),
                  pl.BlockSpec((None, 1, width), lambda b, t: (b, 0, 0))],
        out_specs=[pl.BlockSpec((tt, width), lambda b, t: (b * nt + t, 0)),
                   pl.BlockSpec((None, tail, width), lambda b, t: (b, 0, 0)),
                   pl.BlockSpec((None, 1, width), lambda b, t: (b, 0, 0))],
        out_shape=[jax.ShapeDtypeStruct((m, width), BF16),
                   jax.ShapeDtypeStruct((batch, tail, width), F32),
                   jax.ShapeDtypeStruct((batch, 1, width), F32)],
        scratch_shapes=[pltpu.VMEM((tt + V7X_SUBLANES, width), F32), pltpu.VMEM((1, width), F32)],
        compiler_params=_params(("parallel", "arbitrary")),
        name="lru_branch",
    )(xn, w_lru, conv_w, conv_b, wa, wx, ba, bx, a_logit, conv_state, h0)


def _memkv_kernel(mem_ref, g_ref, wk_ref, wv_ref, kg_ref, k_ref, v_ref):
    nh, n, hd = k_ref.shape
    m = _rms(mem_ref[...], g_ref[...]).astype(BF16)
    k = _dot(m, wk_ref[...])
    v = _dot(m, wv_ref[...])
    for h in range(nh):
        k_ref[h] = _rms(k[:, h * hd:(h + 1) * hd], kg_ref[...])
        v_ref[h] = v[:, h * hd:(h + 1) * hd]


def _memory_kv(mem, g, wk, wv, kg):
    b, n, d = mem.shape
    nh = MEM_HEADS
    hd = wk.shape[1] // nh
    out = jax.ShapeDtypeStruct((b, nh, n, hd), F32)
    out_spec = pl.BlockSpec((None, nh, n, hd), lambda i: (i, 0, 0, 0))
    return pl.pallas_call(
        _memkv_kernel,
        grid=(b,),
        in_specs=[pl.BlockSpec((None, n, d), lambda i: (i, 0, 0)), _const_spec((1, d)), _const_spec(wk.shape),
                  _const_spec(wv.shape), _const_spec((1, hd))],
        out_specs=[out_spec, out_spec],
        out_shape=[out, out],
        compiler_params=_params(("parallel",)),
        name="memory_kv",
    )(mem, g, wk, wv, kg)


def _merge_kernel(x_ref, xn_ref, osb_ref, olru_ref, mk_ref, mv_ref, wqm_ref, qg_ref, wg_ref, bm_ref,
                  wsb_ref, wlru_ref, wmem_ref, wout_ref, o_ref):
    nh, n_mem, hd = mk_ref.shape
    d = o_ref.shape[1]
    xn = xn_ref[...]

    qm = _dot(xn, wqm_ref[...])
    heads = []
    for h in range(nh):
        qh = _rms(qm[:, h * hd:(h + 1) * hd], qg_ref[...]).astype(BF16)
        s = _dot_nt(qh, mk_ref[h].astype(BF16)) / math.sqrt(hd)
        e = jnp.exp(s - jnp.max(s, axis=-1, keepdims=True))
        p = e / jnp.sum(e, axis=-1, keepdims=True)
        heads.append(_dot(p.astype(BF16), mv_ref[h].astype(BF16)))
    omem = jnp.concatenate(heads, axis=1).astype(BF16)

    gates = _sigmoid(_dot(xn, wg_ref[...]) + bm_ref[...])
    merged = (gates[:, :d] * _dot(osb_ref[...], wsb_ref[...])
              + gates[:, d:2 * d] * _dot(olru_ref[...], wlru_ref[...])
              + gates[:, 2 * d:] * _dot(omem, wmem_ref[...]))
    o_ref[...] = x_ref[...] + _dot(merged.astype(BF16), wout_ref[...])


def _merge(x2, xn, osb, olru, mk, mv, wqm, qg, wg, bm, wsb, wlru, wmem, wout, seq):
    m, d = x2.shape
    _, nh, n_mem, hd = mk.shape
    tm = min(TOKEN_TILE, seq)
    per_batch = seq // tm
    tok = pl.BlockSpec((tm, d), lambda i: (i, 0))
    mem_spec = pl.BlockSpec((None, nh, n_mem, hd), lambda i: (i // per_batch, 0, 0, 0))
    return pl.pallas_call(
        _merge_kernel,
        grid=(m // tm,),
        in_specs=[tok, tok, tok, tok, mem_spec, mem_spec, _const_spec(wqm.shape), _const_spec((1, hd)),
                  _const_spec(wg.shape), _const_spec((1, N_BRANCH * d)), _const_spec(wsb.shape),
                  _const_spec(wlru.shape), _const_spec(wmem.shape), _const_spec(wout.shape)],
        out_specs=tok,
        out_shape=jax.ShapeDtypeStruct((m, d), F32),
        compiler_params=_params(("parallel",)),
        name="merge",
    )(x2, xn, osb, olru, mk, mv, wqm, qg, wg, bm, wsb, wlru, wmem, wout)


def _ffn_kernel(x_ref, g_ref, wgate_ref, wup_ref, wdown_ref, o_ref):
    x = x_ref[...]
    xn = _rms(x, g_ref[...]).astype(BF16)
    gate = _dot(xn, wgate_ref[...])
    up = _dot(xn, wup_ref[...])
    hidden = (gate * _sigmoid(gate) * up).astype(BF16)
    o_ref[...] = x + _dot(hidden, wdown_ref[...])


def _ffn(x1, g, wgate, wup, wdown):
    m, d = x1.shape
    tm = min(TOKEN_TILE, m)
    tok = pl.BlockSpec((tm, d), lambda i: (i, 0))
    return pl.pallas_call(
        _ffn_kernel,
        grid=(m // tm,),
        in_specs=[tok, _const_spec((1, d)), _const_spec(wgate.shape), _const_spec(wup.shape),
                  _const_spec(wdown.shape)],
        out_specs=tok,
        out_shape=jax.ShapeDtypeStruct((m, d), F32),
        compiler_params=_params(("parallel",)),
        name="ffn",
    )(x1, g, wgate, wup, wdown)


def _layer(x, past_k, past_v, conv_state, h0, mem_k, mem_v, lw):
    b, t, d = x.shape
    x2 = x.reshape(b * t, d)
    sbw = SB_HEADS * SB_HEAD_DIM
    lru_w = lw["conv_w"].shape[1]
    row = lambda v: v.reshape(1, -1)

    xn, q, k_bf, v_bf, k_new, v_new = _qkv_proj(x2, row(lw["norm_mix_g"]), lw["w_in"][:, :3 * sbw], b, t)

    u = _strict_upper(SB_TILE)
    if past_k is None:
        o_sb = _sb_prompt(q, k_bf, v_bf, u if t >= SB_TILE else _strict_upper(t))
    else:
        o_sb = _sb_sample(q, k_bf, v_bf, past_k, past_v, u, _strict_upper(t))

    lru0 = 3 * sbw
    o_lru, new_conv, h_last = _lru_branch(
        xn, lw["w_in"][:, lru0:lru0 + 2 * lru_w], lw["conv_w"], row(lw["conv_b"]), lw["lru_wa"], lw["lru_wx"],
        row(lw["lru_ba"]), row(lw["lru_bx"]), row(lw["lru_a_logit"]), conv_state, h0.reshape(b, 1, lru_w), b, t)

    qm0 = lru0 + 2 * lru_w
    mem_w = lw["w_br_mem"].shape[0]
    x1 = _merge(x2, xn, o_sb, o_lru, mem_k, mem_v, lw["w_in"][:, qm0:qm0 + mem_w], row(lw["q_norm_g"]),
                lw["w_in"][:, qm0 + mem_w:], row(lw["b_merge"]), lw["w_br_sb"], lw["w_br_lru"], lw["w_br_mem"],
                lw["w_out"], t)
    y = _ffn(x1, row(lw["norm_ffn_g"]), lw["w_ffn_gate"], lw["w_ffn_up"], lw["w_ffn_down"])
    return y.reshape(b, t, d), k_new, v_new, new_conv, h_last.reshape(b, lru_w)


_BF16_WEIGHTS = ("w_in", "lru_wa", "lru_wx", "w_br_sb", "w_br_lru", "w_br_mem", "w_out", "w_ffn_gate",
                 "w_ffn_up", "w_ffn_down")


def kernel(x_prompt, x_sample, mem_prompt, cache_sb_k, cache_sb_v, state_conv, state_lru_h, cache_mem_k, cache_mem_v, norm_mix_g, w_in, b_merge, conv_w, conv_b, lru_wa, lru_ba, lru_wx, lru_bx, lru_a_logit, q_norm_g, k_norm_g, mem_norm_g, w_mem_k, w_mem_v, w_br_sb, w_br_lru, w_br_mem, w_out, norm_ffn_g, w_ffn_gate, w_ffn_up, w_ffn_down):
    depth = w_in.shape[0]
    bp = x_prompt.shape[0]
    lru_w = conv_w.shape[2]
    hp, hs = x_prompt, x_sample
    outs = [[] for _ in range(10)]
    for l in range(depth):
        lw = {
            "norm_mix_g": norm_mix_g[l], "w_in": w_in[l], "b_merge": b_merge[l], "conv_w": conv_w[l],
            "conv_b": conv_b[l], "lru_wa": lru_wa[l], "lru_ba": lru_ba[l], "lru_wx": lru_wx[l],
            "lru_bx": lru_bx[l], "lru_a_logit": lru_a_logit[l], "q_norm_g": q_norm_g[l],
            "w_br_sb": w_br_sb[l], "w_br_lru": w_br_lru[l], "w_br_mem": w_br_mem[l], "w_out": w_out[l],
            "norm_ffn_g": norm_ffn_g[l], "w_ffn_gate": w_ffn_gate[l], "w_ffn_up": w_ffn_up[l],
            "w_ffn_down": w_ffn_down[l],
        }
        for name in _BF16_WEIGHTS:
            lw[name] = lw[name].astype(BF16)
        mk_p, mv_p = _memory_kv(mem_prompt, mem_norm_g[l].reshape(1, -1), w_mem_k[l].astype(BF16),
                                w_mem_v[l].astype(BF16), k_norm_g[l].reshape(1, -1))
        hp, k_p, v_p, c_p, s_p = _layer(hp, None, None, jnp.zeros((bp, CONV_WIDTH - 1, lru_w), F32),
                                        jnp.zeros((bp, lru_w), F32), mk_p, mv_p, lw)
        hs, k_s, v_s, c_s, s_s = _layer(hs, cache_sb_k[l], cache_sb_v[l], state_conv[l], state_lru_h[l],
                                        cache_mem_k[l], cache_mem_v[l], lw)
        for lst, val in zip(outs, (k_p, v_p, c_p, s_p, mk_p, mv_p, k_s, v_s, c_s, s_s)):
            lst.append(val)
    return (hp, hs) + tuple(lst[0][None] if depth == 1 else jnp.stack(lst) for lst in outs)
```

```python
import functools
import math

import jax
import jax.numpy as jnp
from jax import lax
from jax.experimental import pallas as pl
from jax.experimental.pallas import tpu as pltpu

F32 = jnp.float32
BF16 = jnp.bfloat16

EPS = 1e-6
SB_HEADS = 8
SB_HEAD_DIM = 128
LRU_BLOCKS = 8
LRU_C = 8.0
CONV_WIDTH = 4
MEM_HEADS = 4
N_BRANCH = 3

V7X_SUBLANES = 8
V7X_MXU_DIM = 256
V7X_VMEM_LIMIT_BYTES = 56 * 1024 * 1024

SB_TILE = V7X_MXU_DIM
SB_HEADS_PER_STEP = 2
SB_EXIT_LOG = -104.0
TOKEN_TILE = 512
LRU_TILE = 256


def _dot(a, b):
    return jnp.dot(a, b, preferred_element_type=F32)


def _dot_nt(a, b):
    return lax.dot_general(a, b, (((1,), (1,)), ((), ())), preferred_element_type=F32)


def _sigmoid(x):
    return 1.0 / (1.0 + jnp.exp(-x))


def _rms(x, g):
    ms = jnp.mean(x * x, axis=-1, keepdims=True)
    return x * lax.rsqrt(ms + EPS) * g


def _const_spec(shape):
    zeros = (0,) * len(shape)
    return pl.BlockSpec(shape, lambda *_: zeros, pipeline_mode=pl.Buffered(1))


def _params(semantics):
    return pltpu.CompilerParams(dimension_semantics=semantics, vmem_limit_bytes=V7X_VMEM_LIMIT_BYTES)


def _qkv_kernel(x_ref, g_ref, w_ref, xn_ref, q_ref, kb_ref, vb_ref, kf_ref, vf_ref):
    xn = _rms(x_ref[...], g_ref[...]).astype(BF16)
    xn_ref[...] = xn
    bb, nh, tt, hd = q_ref.shape
    width = nh * hd
    q_scale = 1.0 / math.sqrt(hd)
    for part, (b_ref, f_ref) in enumerate(((q_ref, None), (kb_ref, kf_ref), (vb_ref, vf_ref))):
        res = _dot(xn, w_ref[:, part * width:(part + 1) * width])
        for h in range(nh):
            blk = res[:, h * hd:(h + 1) * hd].reshape(bb, tt, hd)
            if f_ref is None:
                blk = blk * q_scale
            else:
                f_ref[:, h] = blk
            b_ref[:, h] = blk.astype(BF16)


def _qkv_proj(x2, g, w_qkv, batch, seq):
    m, d = x2.shape
    nh, hd = SB_HEADS, SB_HEAD_DIM
    tm = min(TOKEN_TILE, m)
    if seq >= tm:
        bb, tt = 1, tm
    else:
        bb, tt = tm // seq, seq
    nt = seq // tt
    head_map = lambda i: (i // nt, 0, i % nt, 0)
    head_spec = pl.BlockSpec((bb, nh, tt, hd), head_map)
    head_bf = jax.ShapeDtypeStruct((batch, nh, seq, hd), BF16)
    head_f32 = jax.ShapeDtypeStruct((batch, nh, seq, hd), F32)
    return pl.pallas_call(
        _qkv_kernel,
        grid=(m // tm,),
        in_specs=[pl.BlockSpec((tm, d), lambda i: (i, 0)), _const_spec((1, d)), _const_spec(w_qkv.shape)],
        out_specs=[pl.BlockSpec((tm, d), lambda i: (i, 0)), head_spec, head_spec, head_spec, head_spec,
                   head_spec],
        out_shape=[jax.ShapeDtypeStruct((m, d), BF16), head_bf, head_bf, head_bf, head_f32, head_f32],
        compiler_params=_params(("parallel",)),
        name="qkv_proj",
    )(x2, g, w_qkv)


def _sb_scores(q, k):
    z = _dot_nt(q, k)
    softplus = jnp.maximum(z, 0.0) + jnp.log(1.0 + jnp.exp(-jnp.abs(z)))
    return z - softplus, -softplus


def _suffix_sums(log_surv, u2):
    hi = log_surv.astype(BF16)
    lo = (log_surv - hi.astype(F32)).astype(BF16)
    return _dot(jnp.concatenate([hi, lo], axis=1), u2)


def _row_sum(x):
    return jnp.sum(x, axis=-1, keepdims=True)


def _sb_tile(q, k, v, u2, c, mask):
    log_beta, log_surv = _sb_scores(q, k)
    if mask is not None:
        log_surv = jnp.where(mask, log_surv, 0.0)
    w = jnp.exp(log_beta + _suffix_sums(log_surv, u2) + c)
    if mask is not None:
        w = jnp.where(mask, w, 0.0)
    return _dot(w.astype(BF16), v), c + _row_sum(log_surv)


def _sb_own_and_previous(q, k2, v2, u2, mask):
    tk = mask.shape[1]
    log_beta, log_surv = _sb_scores(q, k2)
    ls_own = jnp.where(mask, log_surv[:, tk:], 0.0)
    ls_prev = log_surv[:, :tk]
    c_own = _row_sum(ls_own)
    w_own = jnp.where(mask, jnp.exp(log_beta[:, tk:] + _suffix_sums(ls_own, u2)), 0.0)
    w_prev = jnp.exp(log_beta[:, :tk] + _suffix_sums(ls_prev, u2) + c_own)
    w = jnp.concatenate([w_prev, w_own], axis=1).astype(BF16)
    return _dot(w, v2), c_own + _row_sum(ls_prev)


def _causal_mask(tq, tk):
    row = lax.broadcasted_iota(jnp.int32, (tq, tk), 0)
    col = lax.broadcasted_iota(jnp.int32, (tq, tk), 1)
    return col < row


def _max_all(cs):
    return functools.reduce(jnp.maximum, [jnp.max(c) for c in cs])


def _sb_sweep_rest(first_tile, tile_fn, accs, cs):
    def cond(state):
        return jnp.logical_and(state[0] >= 0, state[1] > SB_EXIT_LOG)

    def body(state):
        kb, _, accs, cs = state
        new = [tile_fn(h, kb, cs[h]) for h in range(len(cs))]
        accs = tuple(acc + n[0] for acc, n in zip(accs, new))
        cs = tuple(n[1] for n in new)
        return kb - 1, _max_all(cs), accs, cs

    state = lax.while_loop(cond, body, (first_tile, _max_all(cs), tuple(accs), tuple(cs)))
    return state[2]


def _sb_prompt_kernel(q_ref, k_ref, v_ref, u2_ref, o_ref):
    nh, tq, hd = q_ref.shape
    tk = tq
    qi = pl.program_id(2)
    u2 = u2_ref[...]
    mask = _causal_mask(tq, tk)
    qs = [q_ref[h] for h in range(nh)]

    def own_only():
        outs = [_sb_tile(qs[h], k_ref[h, 0:tk, :], v_ref[h, 0:tk, :], u2, 0.0, mask) for h in range(nh)]
        return tuple(o[0] for o in outs), tuple(o[1] for o in outs)

    def own_and_previous():
        start = pl.multiple_of((qi - 1) * tk, tk)
        outs = [_sb_own_and_previous(qs[h], k_ref[h, pl.ds(start, 2 * tk), :],
                                     v_ref[h, pl.ds(start, 2 * tk), :], u2, mask) for h in range(nh)]
        return tuple(o[0] for o in outs), tuple(o[1] for o in outs)

    accs, cs = lax.cond(qi == 0, own_only, own_and_previous)

    def tile_fn(h, kb, c):
        start = pl.multiple_of(kb * tk, tk)
        return _sb_tile(qs[h], k_ref[h, pl.ds(start, tk), :], v_ref[h, pl.ds(start, tk), :], u2, c, None)

    accs = _sb_sweep_rest(qi - 2, tile_fn, accs, cs)
    for h in range(nh):
        o_ref[:, h * hd:(h + 1) * hd] = accs[h].astype(o_ref.dtype)


def _sb_prompt(q, k, v, u2):
    b, nh, t, hd = q.shape
    tq = u2.shape[1]
    nq = t // tq
    hp = SB_HEADS_PER_STEP
    kv_spec = pl.BlockSpec((None, hp, t, hd), lambda bi, hg, qi: (bi, hg, 0, 0))
    return pl.pallas_call(
        _sb_prompt_kernel,
        grid=(b, nh // hp, nq),
        in_specs=[pl.BlockSpec((None, hp, tq, hd), lambda bi, hg, qi: (bi, hg, qi, 0)), kv_spec, kv_spec,
                  _const_spec(u2.shape)],
        out_specs=pl.BlockSpec((tq, hp * hd), lambda bi, hg, qi: (bi * nq + qi, hg)),
        out_shape=jax.ShapeDtypeStruct((b * t, nh * hd), BF16),
        compiler_params=_params(("parallel", "parallel", "arbitrary")),
        name="sb_prompt",
    )(q, k, v, u2)


def _sb_sample_kernel(q_ref, kn_ref, vn_ref, kp_ref, vp_ref, u2_ref, us2_ref, o_ref):
    nh, t, hd = q_ref.shape
    tk = u2_ref.shape[1]
    n_past = kp_ref.shape[1] // tk
    u2 = u2_ref[...]
    us2 = us2_ref[...]
    mask = _causal_mask(t, t)
    qs = [q_ref[h] for h in range(nh)]

    def tile_fn(h, kb, c):
        start = pl.multiple_of(kb * tk, tk)
        k = kp_ref[h, pl.ds(start, tk), :].astype(BF16)
        v = vp_ref[h, pl.ds(start, tk), :].astype(BF16)
        return _sb_tile(qs[h], k, v, u2, c, None)

    accs, cs = [], []
    for h in range(nh):
        acc_new, c = _sb_tile(qs[h], kn_ref[h], vn_ref[h], us2, 0.0, mask)
        acc_past, c = tile_fn(h, n_past - 1, c)
        accs.append(acc_new + acc_past)
        cs.append(c)
    accs = _sb_sweep_rest(n_past - 2, tile_fn, accs, cs)
    for h in range(nh):
        o_ref[:, h * hd:(h + 1) * hd] = accs[h].astype(o_ref.dtype)


def _sb_sample(q, k_new, v_new, k_past, v_past, u2, u2_small):
    b, nh, t, hd = q.shape
    past = k_past.shape[2]
    new_spec = pl.BlockSpec((None, nh, t, hd), lambda bi: (bi, 0, 0, 0))
    past_spec = pl.BlockSpec((None, nh, past, hd), lambda bi: (bi, 0, 0, 0))
    return pl.pallas_call(
        _sb_sample_kernel,
        grid=(b,),
        in_specs=[new_spec, new_spec, new_spec, past_spec, past_spec, _const_spec(u2.shape),
                  _const_spec(u2_small.shape)],
        out_specs=pl.BlockSpec((t, nh * hd), lambda bi: (bi, 0)),
        out_shape=jax.ShapeDtypeStruct((b * t, nh * hd), BF16),
        compiler_params=_params(("parallel",)),
        name="sb_sample",
    )(q, k_new, v_new, k_past, v_past, u2, u2_small)


def _suffix_sum_matrix(n):
    r = lax.broadcasted_iota(jnp.int32, (2 * n, n), 0) % n
    c = lax.broadcasted_iota(jnp.int32, (2 * n, n), 1)
    return (r > c).astype(BF16)


def _lru_kernel(xn_ref, w_ref, cw_ref, cb_ref, wa_ref, wx_ref, ba_ref, bx_ref, al_ref, cs_ref, h0_ref,
                o_ref, nc_ref, hl_ref, xbuf, hcar):
    tt, width = o_ref.shape
    tail = CONV_WIDTH - 1
    t = pl.program_id(1)
    last = pl.num_programs(1) - 1

    @pl.when(t == 0)
    def _():
        xbuf[0:V7X_SUBLANES, :] = jnp.zeros((V7X_SUBLANES, width), F32)
        xbuf[V7X_SUBLANES - tail:V7X_SUBLANES, :] = cs_ref[...]
        hcar[...] = h0_ref[...]

    xn = xn_ref[...]
    xl = _dot(xn, w_ref[:, :width])
    gl = _dot(xn, w_ref[:, width:])

    xbuf[V7X_SUBLANES:V7X_SUBLANES + tt, :] = xl
    xc = cb_ref[...]
    for i in range(tail):
        xc = xc + xbuf[V7X_SUBLANES - tail + i:V7X_SUBLANES - tail + i + tt, :] * cw_ref[i:i + 1, :]
    xc = xc + xl * cw_ref[tail:tail + 1, :]
    xbuf[0:V7X_SUBLANES, :] = xbuf[tt:tt + V7X_SUBLANES, :]

    @pl.when(t == last)
    def _():
        nc_ref[...] = xbuf[V7X_SUBLANES - tail:V7X_SUBLANES, :]

    nb = wa_ref.shape[0]
    bd = width // nb
    ra, rx = [], []
    for n in range(nb):
        xcn = xc[:, n * bd:(n + 1) * bd].astype(BF16)
        ra.append(_dot(xcn, wa_ref[n]))
        rx.append(_dot(xcn, wx_ref[n]))
    r = _sigmoid(jnp.concatenate(ra, axis=1) + ba_ref[...])
    ig = _sigmoid(jnp.concatenate(rx, axis=1) + bx_ref[...])

    al = al_ref[...]
    log_sig = jnp.minimum(al, 0.0) - jnp.log(1.0 + jnp.exp(-jnp.abs(al)))
    log_a = LRU_C * r * log_sig
    a = jnp.exp(log_a)
    th = jnp.tanh(log_a)
    bterm = jnp.sqrt(-2.0 * th / (1.0 - th)) * (ig * xc)

    groups = tt // V7X_SUBLANES
    a3 = a.reshape(groups, V7X_SUBLANES, width)
    b3 = bterm.reshape(groups, V7X_SUBLANES, width)
    sub = lax.broadcasted_iota(jnp.int32, a3.shape, 1)
    shift = 1
    while shift < V7X_SUBLANES:
        valid = sub >= shift
        b3 = jnp.where(valid, a3 * pltpu.roll(b3, shift, axis=1) + b3, b3)
        a3 = jnp.where(valid, a3 * pltpu.roll(a3, shift, axis=1), a3)
        shift *= 2
    h = hcar[...]
    rows = []
    for gi in range(groups):
        hg = b3[gi] + a3[gi] * h
        rows.append(hg)
        h = hg[V7X_SUBLANES - 1:V7X_SUBLANES, :]
    hcar[...] = h

    @pl.when(t == last)
    def _():
        hl_ref[...] = h

    hs = jnp.concatenate(rows, axis=0)
    gelu = 0.5 * gl * (1.0 + jnp.tanh(0.7978845608028654 * (gl + 0.044715 * (gl * gl * gl))))
    o_ref[...] = (hs * gelu).astype(o_ref.dtype)


def _lru_branch(xn, w_lru, conv_w, conv_b, wa, wx, ba, bx, a_logit, conv_state, h0, batch, seq):
    m, d = xn.shape
    width = w_lru.shape[1] // 2
    tail = CONV_WIDTH - 1
    tt = min(LRU_TILE, seq)
    nt = seq // tt
    return pl.pallas_call(
        _lru_kernel,
        grid=(batch, nt),
        in_specs=[pl.BlockSpec((tt, d), lambda b, t: (b * nt + t, 0)), _const_spec(w_lru.shape),
                  _const_spec(conv_w.shape), _const_spec((1, width)), _const_spec(wa.shape),
                  _const_spec(wx.shape), _const_spec((1, width)), _const_spec((1, width)),
                  _const_spec((1, width)),
                  pl.BlockSpec((None, tail, width), lambda b, t: (b, 0, 0)),
                  pl.BlockSpec((None, 1, width), lambda b, t: (b, 0, 0))],
        out_specs=[pl.BlockSpec((tt, width), lambda b, t: (b * nt + t, 0)),
                   pl.BlockSpec((None, tail, width), lambda b, t: (b, 0, 0)),
                   pl.BlockSpec((None, 1, width), lambda b, t: (b, 0, 0))],
        out_shape=[jax.ShapeDtypeStruct((m, width), BF16),
                   jax.ShapeDtypeStruct((batch, tail, width), F32),
                   jax.ShapeDtypeStruct((batch, 1, width), F32)],
        scratch_shapes=[pltpu.VMEM((tt + V7X_SUBLANES, width), F32), pltpu.VMEM((1, width), F32)],
        compiler_params=_params(("parallel", "arbitrary")),
        name="lru_branch",
    )(xn, w_lru, conv_w, conv_b, wa, wx, ba, bx, a_logit, conv_state, h0)


def _memkv_kernel(mem_ref, g_ref, wk_ref, wv_ref, kg_ref, k_ref, v_ref):
    nh, n, hd = k_ref.shape
    m = _rms(mem_ref[...], g_ref[...]).astype(BF16)
    k = _dot(m, wk_ref[...])
    v = _dot(m, wv_ref[...])
    for h in range(nh):
        k_ref[h] = _rms(k[:, h * hd:(h + 1) * hd], kg_ref[...])
        v_ref[h] = v[:, h * hd:(h + 1) * hd]


def _memory_kv(mem, g, wk, wv, kg):
    b, n, d = mem.shape
    nh = MEM_HEADS
    hd = wk.shape[1] // nh
    out = jax.ShapeDtypeStruct((b, nh, n, hd), F32)
    out_spec = pl.BlockSpec((None, nh, n, hd), lambda i: (i, 0, 0, 0))
    return pl.pallas_call(
        _memkv_kernel,
        grid=(b,),
        in_specs=[pl.BlockSpec((None, n, d), lambda i: (i, 0, 0)), _const_spec((1, d)), _const_spec(wk.shape),
                  _const_spec(wv.shape), _const_spec((1, hd))],
        out_specs=[out_spec, out_spec],
        out_shape=[out, out],
        compiler_params=_params(("parallel",)),
        name="memory_kv",
    )(mem, g, wk, wv, kg)


def _merge_kernel(x_ref, xn_ref, osb_ref, olru_ref, mk_ref, mv_ref, wqm_ref, qg_ref, wg_ref, bm_ref,
                  wsb_ref, wlru_ref, wmem_ref, wout_ref, o_ref):
    nh, n_mem, hd = mk_ref.shape
    d = o_ref.shape[1]
    xn = xn_ref[...]

    qm = _dot(xn, wqm_ref[...])
    heads = []
    for h in range(nh):
        qh = _rms(qm[:, h * hd:(h + 1) * hd], qg_ref[...]).astype(BF16)
        s = _dot_nt(qh, mk_ref[h].astype(BF16)) / math.sqrt(hd)
        e = jnp.exp(s - jnp.max(s, axis=-1, keepdims=True))
        p = e / jnp.sum(e, axis=-1, keepdims=True)
        heads.append(_dot(p.astype(BF16), mv_ref[h].astype(BF16)))
    omem = jnp.concatenate(heads, axis=1).astype(BF16)

    gates = _sigmoid(_dot(xn, wg_ref[...]) + bm_ref[...])
    merged = (gates[:, :d] * _dot(osb_ref[...], wsb_ref[...])
              + gates[:, d:2 * d] * _dot(olru_ref[...], wlru_ref[...])
              + gates[:, 2 * d:] * _dot(omem, wmem_ref[...]))
    o_ref[...] = x_ref[...] + _dot(merged.astype(BF16), wout_ref[...])


def _merge(x2, xn, osb, olru, mk, mv, wqm, qg, wg, bm, wsb, wlru, wmem, wout, seq):
    m, d = x2.shape
    _, nh, n_mem, hd = mk.shape
    tm = min(TOKEN_TILE, seq)
    per_batch = seq // tm
    tok = pl.BlockSpec((tm, d), lambda i: (i, 0))
    mem_spec = pl.BlockSpec((None, nh, n_mem, hd), lambda i: (i // per_batch, 0, 0, 0))
    return pl.pallas_call(
        _merge_kernel,
        grid=(m // tm,),
        in_specs=[tok, tok, tok, tok, mem_spec, mem_spec, _const_spec(wqm.shape), _const_spec((1, hd)),
                  _const_spec(wg.shape), _const_spec((1, N_BRANCH * d)), _const_spec(wsb.shape),
                  _const_spec(wlru.shape), _const_spec(wmem.shape), _const_spec(wout.shape)],
        out_specs=tok,
        out_shape=jax.ShapeDtypeStruct((m, d), F32),
        compiler_params=_params(("parallel",)),
        name="merge",
    )(x2, xn, osb, olru, mk, mv, wqm, qg, wg, bm, wsb, wlru, wmem, wout)


def _ffn_kernel(x_ref, g_ref, wgate_ref, wup_ref, wdown_ref, o_ref):
    x = x_ref[...]
    xn = _rms(x, g_ref[...]).astype(BF16)
    gate = _dot(xn, wgate_ref[...])
    up = _dot(xn, wup_ref[...])
    hidden = (gate * _sigmoid(gate) * up).astype(BF16)
    o_ref[...] = x + _dot(hidden, wdown_ref[...])


def _ffn(x1, g, wgate, wup, wdown):
    m, d = x1.shape
    tm = min(TOKEN_TILE, m)
    tok = pl.BlockSpec((tm, d), lambda i: (i, 0))
    return pl.pallas_call(
        _ffn_kernel,
        grid=(m // tm,),
        in_specs=[tok, _const_spec((1, d)), _const_spec(wgate.shape), _const_spec(wup.shape),
                  _const_spec(wdown.shape)],
        out_specs=tok,
        out_shape=jax.ShapeDtypeStruct((m, d), F32),
        compiler_params=_params(("parallel",)),
        name="ffn",
    )(x1, g, wgate, wup, wdown)


def _layer(x, past_k, past_v, conv_state, h0, mem_k, mem_v, lw):
    b, t, d = x.shape
    x2 = x.reshape(b * t, d)
    sbw = SB_HEADS * SB_HEAD_DIM
    lru_w = lw["conv_w"].shape[1]
    row = lambda v: v.reshape(1, -1)

    xn, q, k_bf, v_bf, k_new, v_new = _qkv_proj(x2, row(lw["norm_mix_g"]), lw["w_in"][:, :3 * sbw], b, t)

    if past_k is None:
        o_sb = _sb_prompt(q, k_bf, v_bf, _suffix_sum_matrix(min(SB_TILE, t)))
    else:
        o_sb = _sb_sample(q, k_bf, v_bf, past_k, past_v, _suffix_sum_matrix(SB_TILE), _suffix_sum_matrix(t))

    lru0 = 3 * sbw
    o_lru, new_conv, h_last = _lru_branch(
        xn, lw["w_in"][:, lru0:lru0 + 2 * lru_w], lw["conv_w"], row(lw["conv_b"]), lw["lru_wa"], lw["lru_wx"],
        row(lw["lru_ba"]), row(lw["lru_bx"]), row(lw["lru_a_logit"]), conv_state, h0.reshape(b, 1, lru_w), b, t)

    qm0 = lru0 + 2 * lru_w
    mem_w = lw["w_br_mem"].shape[0]
    x1 = _merge(x2, xn, o_sb, o_lru, mem_k, mem_v, lw["w_in"][:, qm0:qm0 + mem_w], row(lw["q_norm_g"]),
                lw["w_in"][:, qm0 + mem_w:], row(lw["b_merge"]), lw["w_br_sb"], lw["w_br_lru"], lw["w_br_mem"],
                lw["w_out"], t)
    y = _ffn(x1, row(lw["norm_ffn_g"]), lw["w_ffn_gate"], lw["w_ffn_up"], lw["w_ffn_down"])
    return y.reshape(b, t, d), k_new, v_new, new_conv, h_last.reshape(b, lru_w)


_BF16_WEIGHTS = ("w_in", "lru_wa", "lru_wx", "w_br_sb", "w_br_lru", "w_br_mem", "w_out", "w_ffn_gate",
                 "w_ffn_up", "w_ffn_down")


def kernel(x_prompt, x_sample, mem_prompt, cache_sb_k, cache_sb_v, state_conv, state_lru_h, cache_mem_k, cache_mem_v, norm_mix_g, w_in, b_merge, conv_w, conv_b, lru_wa, lru_ba, lru_wx, lru_bx, lru_a_logit, q_norm_g, k_norm_g, mem_norm_g, w_mem_k, w_mem_v, w_br_sb, w_br_lru, w_br_mem, w_out, norm_ffn_g, w_ffn_gate, w_ffn_up, w_ffn_down):
    depth = w_in.shape[0]
    bp = x_prompt.shape[0]
    lru_w = conv_w.shape[2]
    hp, hs = x_prompt, x_sample
    outs = [[] for _ in range(10)]
    for l in range(depth):
        lw = {
            "norm_mix_g": norm_mix_g[l], "w_in": w_in[l], "b_merge": b_merge[l], "conv_w": conv_w[l],
            "conv_b": conv_b[l], "lru_wa": lru_wa[l], "lru_ba": lru_ba[l], "lru_wx": lru_wx[l],
            "lru_bx": lru_bx[l], "lru_a_logit": lru_a_logit[l], "q_norm_g": q_norm_g[l],
            "w_br_sb": w_br_sb[l], "w_br_lru": w_br_lru[l], "w_br_mem": w_br_mem[l], "w_out": w_out[l],
            "norm_ffn_g": norm_ffn_g[l], "w_ffn_gate": w_ffn_gate[l], "w_ffn_up": w_ffn_up[l],
            "w_ffn_down": w_ffn_down[l],
        }
        for name in _BF16_WEIGHTS:
            lw[name] = lw[name].astype(BF16)
        mk_p, mv_p = _memory_kv(mem_prompt, mem_norm_g[l].reshape(1, -1), w_mem_k[l].astype(BF16),
                                w_mem_v[l].astype(BF16), k_norm_g[l].reshape(1, -1))
        hp, k_p, v_p, c_p, s_p = _layer(hp, None, None, jnp.zeros((bp, CONV_WIDTH - 1, lru_w), F32),
                                        jnp.zeros((bp, lru_w), F32), mk_p, mv_p, lw)
        hs, k_s, v_s, c_s, s_s = _layer(hs, cache_sb_k[l], cache_sb_v[l], state_conv[l], state_lru_h[l],
                                        cache_mem_k[l], cache_mem_v[l], lw)
        for lst, val in zip(outs, (k_p, v_p, c_p, s_p, mk_p, mv_p, k_s, v_s, c_s, s_s)):
            lst.append(val)
    return (hp, hs) + tuple(lst[0][None] if depth == 1 else jnp.stack(lst) for lst in outs)
```

```python
import functools
import math

import jax
import jax.numpy as jnp
from jax import lax
from jax.experimental import pallas as pl
from jax.experimental.pallas import tpu as pltpu

F32 = jnp.float32
BF16 = jnp.bfloat16

EPS = 1e-6
LOG2E = 1.4426950408889634
SB_HEADS = 8
SB_HEAD_DIM = 128
LRU_BLOCKS = 8
LRU_C = 8.0
CONV_WIDTH = 4
MEM_HEADS = 4
N_BRANCH = 3

V7X_SUBLANES = 8
V7X_MXU_DIM = 256
V7X_VMEM_LIMIT_BYTES = 56 * 1024 * 1024

SB_TILE = V7X_MXU_DIM
SB_HEADS_PER_STEP = 4
SB_EXIT_DROP = 150.0
TOKEN_TILE = 512
MERGE_SAMPLE_BATCHES = 4


def _dot(a, b):
    return jnp.dot(a, b, preferred_element_type=F32)


def _dot_nt(a, b):
    return lax.dot_general(a, b, (((1,), (1,)), ((), ())), preferred_element_type=F32)


def _sigmoid(x):
    return 1.0 / (1.0 + jnp.exp(-x))


def _rms(x, g):
    ms = jnp.mean(x * x, axis=-1, keepdims=True)
    return x * lax.rsqrt(ms + EPS) * g


def _const_spec(shape):
    zeros = (0,) * len(shape)
    return pl.BlockSpec(shape, lambda *_: zeros, pipeline_mode=pl.Buffered(1))


def _params(semantics):
    return pltpu.CompilerParams(dimension_semantics=semantics, vmem_limit_bytes=V7X_VMEM_LIMIT_BYTES)


def _qkv_kernel(x_ref, g_ref, w_ref, xn_ref, q_ref, kb_ref, vb_ref, kf_ref, vf_ref):
    xn = _rms(x_ref[...], g_ref[...]).astype(BF16)
    xn_ref[...] = xn
    bb, nh, tt, hd = q_ref.shape
    width = nh * hd
    q_scale = LOG2E / math.sqrt(hd)
    for part, (b_ref, f_ref) in enumerate(((q_ref, None), (kb_ref, kf_ref), (vb_ref, vf_ref))):
        res = _dot(xn, w_ref[:, part * width:(part + 1) * width])
        for h in range(nh):
            blk = res[:, h * hd:(h + 1) * hd].reshape(bb, tt, hd)
            if f_ref is None:
                blk = blk * q_scale
            else:
                f_ref[:, h] = blk
            b_ref[:, h] = blk.astype(BF16)


def _qkv_proj(x2, g, w_qkv, batch, seq):
    m, d = x2.shape
    nh, hd = SB_HEADS, SB_HEAD_DIM
    tm = min(TOKEN_TILE, m)
    if seq >= tm:
        bb, tt = 1, tm
    else:
        bb, tt = tm // seq, seq
    nt = seq // tt
    if bb == 1:
        xn_spec = pl.BlockSpec((tm, d), lambda i: (i % nt, i // nt))
        xn_shape = jax.ShapeDtypeStruct((seq, batch * d), BF16)
    else:
        xn_spec = pl.BlockSpec((tm, d), lambda i: (i, 0))
        xn_shape = jax.ShapeDtypeStruct((m, d), BF16)
    head_spec = pl.BlockSpec((bb, nh, tt, hd), lambda i: (i // nt, 0, i % nt, 0))
    head_bf = jax.ShapeDtypeStruct((batch, nh, seq, hd), BF16)
    head_f32 = jax.ShapeDtypeStruct((batch, nh, seq, hd), F32)
    return bb == 1, pl.pallas_call(
        _qkv_kernel,
        grid=(m // tm,),
        in_specs=[pl.BlockSpec((tm, d), lambda i: (i, 0)), _const_spec((1, d)), _const_spec(w_qkv.shape)],
        out_specs=[xn_spec, head_spec, head_spec, head_spec, head_spec, head_spec],
        out_shape=[xn_shape, head_bf, head_bf, head_bf, head_f32, head_f32],
        compiler_params=_params(("parallel",)),
        name="qkv_proj",
    )(x2, g, w_qkv)


def _sb_scores(q, k):
    z = _dot_nt(q, k)
    neg_abs = lax.bitcast_convert_type(lax.bitcast_convert_type(z, jnp.uint32) | jnp.uint32(0x80000000), F32)
    softplus = jnp.maximum(z, 0.0) + jnp.log2(1.0 + jnp.exp2(neg_abs))
    return z - softplus, softplus


def _suffix_sums(drop, u2):
    hi = drop.astype(BF16)
    lo = (drop - hi.astype(F32)).astype(BF16)
    return _dot(jnp.concatenate([hi, lo], axis=1), u2)


def _row_sum(x):
    return jnp.sum(x, axis=-1, keepdims=True)


def _sb_tile(q, k, v, u2, c, mask):
    log_beta, drop = _sb_scores(q, k)
    if mask is not None:
        drop = jnp.where(mask, drop, 0.0)
    w = jnp.exp2(log_beta - _suffix_sums(drop, u2) - c)
    if mask is not None:
        w = jnp.where(mask, w, 0.0)
    return _dot(w.astype(BF16), v), c + _row_sum(drop)


def _sb_own_and_previous(q, k2, v2, u2, mask):
    tk = mask.shape[1]
    log_beta, drop = _sb_scores(q, k2)
    drop_own = jnp.where(mask, drop[:, tk:], 0.0)
    drop_prev = drop[:, :tk]
    c_own = _row_sum(drop_own)
    w_own = jnp.where(mask, jnp.exp2(log_beta[:, tk:] - _suffix_sums(drop_own, u2)), 0.0)
    w_prev = jnp.exp2(log_beta[:, :tk] - _suffix_sums(drop_prev, u2) - c_own)
    w = jnp.concatenate([w_prev, w_own], axis=1).astype(BF16)
    return _dot(w, v2), c_own + _row_sum(drop_prev)


def _causal_mask(tq, tk):
    row = lax.broadcasted_iota(jnp.int32, (tq, tk), 0)
    col = lax.broadcasted_iota(jnp.int32, (tq, tk), 1)
    return col < row


def _min_all(cs):
    return functools.reduce(jnp.minimum, [jnp.min(c) for c in cs])


def _sb_sweep_rest(first_tile, tile_fn, accs, cs):
    def cond(state):
        return jnp.logical_and(state[0] >= 0, state[1] < SB_EXIT_DROP)

    def body(state):
        kb, _, accs, cs = state
        new = [tile_fn(h, kb, cs[h]) for h in range(len(cs))]
        accs = tuple(acc + n[0] for acc, n in zip(accs, new))
        cs = tuple(n[1] for n in new)
        return kb - 1, _min_all(cs), accs, cs

    state = lax.while_loop(cond, body, (first_tile, _min_all(cs), tuple(accs), tuple(cs)))
    return state[2]


def _sb_prompt_kernel(q_ref, k_ref, v_ref, u2_ref, o_ref):
    nh, tq, hd = q_ref.shape
    tk = tq
    qi = pl.program_id(2)
    u2 = u2_ref[...]
    mask = _causal_mask(tq, tk)
    qs = [q_ref[h] for h in range(nh)]

    def own_only():
        outs = [_sb_tile(qs[h], k_ref[h, 0:tk, :], v_ref[h, 0:tk, :], u2, 0.0, mask) for h in range(nh)]
        return tuple(o[0] for o in outs), tuple(o[1] for o in outs)

    def own_and_previous():
        start = pl.multiple_of((qi - 1) * tk, tk)
        outs = [_sb_own_and_previous(qs[h], k_ref[h, pl.ds(start, 2 * tk), :],
                                     v_ref[h, pl.ds(start, 2 * tk), :], u2, mask) for h in range(nh)]
        return tuple(o[0] for o in outs), tuple(o[1] for o in outs)

    accs, cs = lax.cond(qi == 0, own_only, own_and_previous)

    def tile_fn(h, kb, c):
        start = pl.multiple_of(kb * tk, tk)
        return _sb_tile(qs[h], k_ref[h, pl.ds(start, tk), :], v_ref[h, pl.ds(start, tk), :], u2, c, None)

    accs = _sb_sweep_rest(qi - 2, tile_fn, accs, cs)
    for h in range(nh):
        o_ref[:, h * hd:(h + 1) * hd] = accs[h].astype(o_ref.dtype)


def _sb_prompt(q, k, v, u2):
    b, nh, t, hd = q.shape
    tq = u2.shape[1]
    nq = t // tq
    hp = SB_HEADS_PER_STEP
    kv_spec = pl.BlockSpec((None, hp, t, hd), lambda bi, hg, qi: (bi, hg, 0, 0))
    return pl.pallas_call(
        _sb_prompt_kernel,
        grid=(b, nh // hp, nq),
        in_specs=[pl.BlockSpec((None, hp, tq, hd), lambda bi, hg, qi: (bi, hg, qi, 0)), kv_spec, kv_spec,
                  _const_spec(u2.shape)],
        out_specs=pl.BlockSpec((tq, hp * hd), lambda bi, hg, qi: (bi * nq + qi, hg)),
        out_shape=jax.ShapeDtypeStruct((b * t, nh * hd), BF16),
        compiler_params=_params(("parallel", "parallel", "arbitrary")),
        name="sb_prompt",
    )(q, k, v, u2)


def _sb_sample_kernel(q_ref, kn_ref, vn_ref, kp_ref, vp_ref, u2_ref, us2_ref, o_ref):
    nh, t, hd = q_ref.shape
    tk = u2_ref.shape[1]
    n_past = kp_ref.shape[1] // tk
    u2 = u2_ref[...]
    us2 = us2_ref[...]
    mask = _causal_mask(t, t)
    qs = [q_ref[h] for h in range(nh)]

    def tile_fn(h, kb, c):
        start = pl.multiple_of(kb * tk, tk)
        k = kp_ref[h, pl.ds(start, tk), :].astype(BF16)
        v = vp_ref[h, pl.ds(start, tk), :].astype(BF16)
        return _sb_tile(qs[h], k, v, u2, c, None)

    accs, cs = [], []
    for h in range(nh):
        acc_new, c = _sb_tile(qs[h], kn_ref[h], vn_ref[h], us2, 0.0, mask)
        acc_past, c = tile_fn(h, n_past - 1, c)
        accs.append(acc_new + acc_past)
        cs.append(c)
    accs = _sb_sweep_rest(n_past - 2, tile_fn, accs, cs)
    for h in range(nh):
        o_ref[:, h * hd:(h + 1) * hd] = accs[h].astype(o_ref.dtype)


def _sb_sample(q, k_new, v_new, k_past, v_past, u2, u2_small):
    b, nh, t, hd = q.shape
    past = k_past.shape[2]
    new_spec = pl.BlockSpec((None, nh, t, hd), lambda bi: (bi, 0, 0, 0))
    past_spec = pl.BlockSpec((None, nh, past, hd), lambda bi: (bi, 0, 0, 0))
    return pl.pallas_call(
        _sb_sample_kernel,
        grid=(b,),
        in_specs=[new_spec, new_spec, new_spec, past_spec, past_spec, _const_spec(u2.shape),
                  _const_spec(u2_small.shape)],
        out_specs=pl.BlockSpec((t, nh * hd), lambda bi: (bi, 0)),
        out_shape=jax.ShapeDtypeStruct((b * t, nh * hd), BF16),
        compiler_params=_params(("parallel",)),
        name="sb_sample",
    )(q, k_new, v_new, k_past, v_past, u2, u2_small)


def _suffix_sum_matrix(n):
    r = lax.broadcasted_iota(jnp.int32, (2 * n, n), 0) % n
    c = lax.broadcasted_iota(jnp.int32, (2 * n, n), 1)
    return (r > c).astype(BF16)


def _lru_kernel(xn_ref, w_ref, cw_ref, cb_ref, wg_ref, ba_ref, bx_ref, al_ref, cs_ref, h0_ref,
                o_ref, nc_ref, hl_ref, xbuf, hcar):
    rows, width = o_ref.shape
    batch = hcar.shape[0]
    tail_rows = (CONV_WIDTH - 1) * batch
    i = pl.program_id(0)
    last = pl.num_programs(0) - 1

    @pl.when(i == 0)
    def _():
        xbuf[0:tail_rows, :] = cs_ref[...]
        hcar[...] = h0_ref[...]

    xn = xn_ref[...]
    xl = _dot(xn, w_ref[:, :width])
    gl = _dot(xn, w_ref[:, width:])

    xbuf[tail_rows:tail_rows + rows, :] = xl
    xc = cb_ref[...]
    for tap in range(CONV_WIDTH - 1):
        xc = xc + xbuf[tap * batch:tap * batch + rows, :] * cw_ref[tap:tap + 1, :]
    xc = xc + xl * cw_ref[CONV_WIDTH - 1:CONV_WIDTH, :]
    xbuf[0:tail_rows, :] = xbuf[rows:rows + tail_rows, :]

    @pl.when(i == last)
    def _():
        nc_ref[...] = xbuf[0:tail_rows, :]

    nb, bd, _ = wg_ref.shape
    ra, rx = [], []
    for n in range(nb):
        gates = _dot(xc[:, n * bd:(n + 1) * bd].astype(BF16), wg_ref[n])
        ra.append(gates[:, :bd])
        rx.append(gates[:, bd:])
    r = _sigmoid(jnp.concatenate(ra, axis=1) + ba_ref[...])
    ig = _sigmoid(jnp.concatenate(rx, axis=1) + bx_ref[...])

    al = al_ref[...]
    log_sig = jnp.minimum(al, 0.0) - jnp.log(1.0 + jnp.exp(-jnp.abs(al)))
    log_a = LRU_C * r * log_sig
    a = jnp.exp(log_a)
    th = jnp.tanh(log_a)
    m2 = -2.0 * th / (1.0 - th)
    mult = jnp.where(m2 > 0.0, m2 * lax.rsqrt(m2), 0.0)
    bterm = mult * (ig * xc)

    h = hcar[...]
    hs = []
    for s in range(rows // batch):
        h = a[s * batch:(s + 1) * batch] * h + bterm[s * batch:(s + 1) * batch]
        hs.append(h)
    hcar[...] = h

    @pl.when(i == last)
    def _():
        hl_ref[...] = h

    gelu = 0.5 * gl * (1.0 + jnp.tanh(0.7978845608028654 * (gl + 0.044715 * (gl * gl * gl))))
    o_ref[...] = (jnp.concatenate(hs, axis=0) * gelu).astype(o_ref.dtype)


def _lru_branch(xn_tb, w_lru, conv_w, conv_b, w_gates, ba, bx, a_logit, conv_state_tb, h0, batch, seq):
    m, d = xn_tb.shape
    width = w_lru.shape[1] // 2
    tail_rows = (CONV_WIDTH - 1) * batch
    assert batch % V7X_SUBLANES == 0
    steps = max(1, min(seq, TOKEN_TILE // batch))
    rows = steps * batch
    return pl.pallas_call(
        _lru_kernel,
        grid=(seq // steps,),
        in_specs=[pl.BlockSpec((rows, d), lambda i: (i, 0)), _const_spec(w_lru.shape), _const_spec(conv_w.shape),
                  _const_spec((1, width)), _const_spec(w_gates.shape), _const_spec((1, width)),
                  _const_spec((1, width)), _const_spec((1, width)), _const_spec((tail_rows, width)),
                  _const_spec((batch, width))],
        out_specs=[pl.BlockSpec((rows, width), lambda i: (i, 0)),
                   pl.BlockSpec((tail_rows, width), lambda i: (0, 0)),
                   pl.BlockSpec((batch, width), lambda i: (0, 0))],
        out_shape=[jax.ShapeDtypeStruct((m, width), BF16),
                   jax.ShapeDtypeStruct((tail_rows, width), F32),
                   jax.ShapeDtypeStruct((batch, width), F32)],
        scratch_shapes=[pltpu.VMEM((rows + tail_rows, width), F32), pltpu.VMEM((batch, width), F32)],
        compiler_params=_params(("arbitrary",)),
        name="lru_branch",
    )(xn_tb, w_lru, conv_w, conv_b, w_gates, ba, bx, a_logit, conv_state_tb, h0)


def _memkv_kernel(mem_ref, g_ref, wk_ref, wv_ref, kg_ref, k_ref, v_ref):
    nh, n, hd = k_ref.shape
    m = _rms(mem_ref[...], g_ref[...]).astype(BF16)
    k = _dot(m, wk_ref[...])
    v = _dot(m, wv_ref[...])
    for h in range(nh):
        k_ref[h] = _rms(k[:, h * hd:(h + 1) * hd], kg_ref[...])
        v_ref[h] = v[:, h * hd:(h + 1) * hd]


def _memory_kv(mem, g, wk, wv, kg):
    b, n, d = mem.shape
    nh = MEM_HEADS
    hd = wk.shape[1] // nh
    out = jax.ShapeDtypeStruct((b, nh, n, hd), F32)
    out_spec = pl.BlockSpec((None, nh, n, hd), lambda i: (i, 0, 0, 0))
    return pl.pallas_call(
        _memkv_kernel,
        grid=(b,),
        in_specs=[pl.BlockSpec((None, n, d), lambda i: (i, 0, 0)), _const_spec((1, d)), _const_spec(wk.shape),
                  _const_spec(wv.shape), _const_spec((1, hd))],
        out_specs=[out_spec, out_spec],
        out_shape=[out, out],
        compiler_params=_params(("parallel",)),
        name="memory_kv",
    )(mem, g, wk, wv, kg)


def _merge_kernel(x_ref, xn_ref, osb_ref, olru_ref, mk_ref, mv_ref, wqm_ref, qg_ref, wg_ref, bm_ref,
                  wsb_ref, wlru_ref, wmem_ref, wout_ref, o_ref):
    nb, nh, n_mem, hd = mk_ref.shape
    tm, d = o_ref.shape
    per_seq = tm // nb
    xn = xn_ref[...]

    qm = _dot(xn, wqm_ref[...])
    seqs = []
    for j in range(nb):
        heads = []
        for h in range(nh):
            qh = _rms(qm[j * per_seq:(j + 1) * per_seq, h * hd:(h + 1) * hd], qg_ref[...]).astype(BF16)
            s = _dot_nt(qh, mk_ref[j, h].astype(BF16)) / math.sqrt(hd)
            e = jnp.exp(s - jnp.max(s, axis=-1, keepdims=True))
            p = e / jnp.sum(e, axis=-1, keepdims=True)
            heads.append(_dot(p.astype(BF16), mv_ref[j, h].astype(BF16)))
        seqs.append(jnp.concatenate(heads, axis=1))
    omem = jnp.concatenate(seqs, axis=0).astype(BF16)

    gates = _sigmoid(_dot(xn, wg_ref[...]) + bm_ref[...])
    merged = (gates[:, :d] * _dot(osb_ref[...], wsb_ref[...])
              + gates[:, d:2 * d] * _dot(olru_ref[...], wlru_ref[...])
              + gates[:, 2 * d:] * _dot(omem, wmem_ref[...]))
    o_ref[...] = x_ref[...] + _dot(merged.astype(BF16), wout_ref[...])


def _merge(x2, xn, osb, olru, mk, mv, wqm, qg, wg, bm, wsb, wlru, wmem, wout, seq, time_major):
    m, d = x2.shape
    _, nh, n_mem, hd = mk.shape
    lru_w = wlru.shape[0]
    tok_of = lambda width: pl.BlockSpec((tm, width), lambda i: (i, 0))
    if time_major:
        tm, nb = min(TOKEN_TILE, seq), 1
        per_batch = seq // tm
        tb_of = lambda width: pl.BlockSpec((tm, width), lambda i: (i % per_batch, i // per_batch))
        mem_spec = pl.BlockSpec((nb, nh, n_mem, hd), lambda i: (i // per_batch, 0, 0, 0))
    else:
        nb = MERGE_SAMPLE_BATCHES
        tm = nb * seq
        tb_of = tok_of
        mem_spec = pl.BlockSpec((nb, nh, n_mem, hd), lambda i: (i, 0, 0, 0))
    return pl.pallas_call(
        _merge_kernel,
        grid=(m // tm,),
        in_specs=[tok_of(d), tb_of(d), tok_of(osb.shape[-1]), tb_of(lru_w), mem_spec, mem_spec,
                  _const_spec(wqm.shape), _const_spec((1, hd)), _const_spec(wg.shape),
                  _const_spec((1, N_BRANCH * d)), _const_spec(wsb.shape), _const_spec(wlru.shape),
                  _const_spec(wmem.shape), _const_spec(wout.shape)],
        out_specs=tok_of(d),
        out_shape=jax.ShapeDtypeStruct((m, d), F32),
        compiler_params=_params(("parallel",)),
        name="merge",
    )(x2, xn, osb, olru, mk, mv, wqm, qg, wg, bm, wsb, wlru, wmem, wout)


def _ffn_kernel(x_ref, g_ref, wgate_ref, wup_ref, wdown_ref, o_ref):
    x = x_ref[...]
    xn = _rms(x, g_ref[...]).astype(BF16)
    gate = _dot(xn, wgate_ref[...])
    up = _dot(xn, wup_ref[...])
    hidden = (gate * _sigmoid(gate) * up).astype(BF16)
    o_ref[...] = x + _dot(hidden, wdown_ref[...])


def _ffn(x1, g, wgate, wup, wdown):
    m, d = x1.shape
    tm = min(TOKEN_TILE, m)
    tok = pl.BlockSpec((tm, d), lambda i: (i, 0))
    return pl.pallas_call(
        _ffn_kernel,
        grid=(m // tm,),
        in_specs=[tok, _const_spec((1, d)), _const_spec(wgate.shape), _const_spec(wup.shape),
                  _const_spec(wdown.shape)],
        out_specs=tok,
        out_shape=jax.ShapeDtypeStruct((m, d), F32),
        compiler_params=_params(("parallel",)),
        name="ffn",
    )(x1, g, wgate, wup, wdown)


def _time_major(a, batch, seq):
    return a.reshape(batch, seq, -1).transpose(1, 0, 2).reshape(seq * batch, -1)


def _layer(x, past_k, past_v, conv_state, h0, mem_k, mem_v, lw):
    b, t, d = x.shape
    x2 = x.reshape(b * t, d)
    sbw = SB_HEADS * SB_HEAD_DIM
    lru_w = lw["conv_w"].shape[1]
    tail = CONV_WIDTH - 1
    row = lambda v: v.reshape(1, -1)

    time_major, (xn, q, k_bf, v_bf, k_new, v_new) = _qkv_proj(x2, row(lw["norm_mix_g"]),
                                                              lw["w_in"][:, :3 * sbw], b, t)

    if past_k is None:
        o_sb = _sb_prompt(q, k_bf, v_bf, _suffix_sum_matrix(min(SB_TILE, t)))
    else:
        o_sb = _sb_sample(q, k_bf, v_bf, past_k, past_v, _suffix_sum_matrix(SB_TILE), _suffix_sum_matrix(t))

    lru0 = 3 * sbw
    xn_tb = xn.reshape(t * b, d) if time_major else _time_major(xn, b, t)
    w_gates = jnp.concatenate([lw["lru_wa"], lw["lru_wx"]], axis=2)
    o_lru_tb, new_conv_tb, h_last = _lru_branch(
        xn_tb, lw["w_in"][:, lru0:lru0 + 2 * lru_w], lw["conv_w"], row(lw["conv_b"]), w_gates,
        row(lw["lru_ba"]), row(lw["lru_bx"]), row(lw["lru_a_logit"]),
        conv_state.transpose(1, 0, 2).reshape(tail * b, lru_w), h0, b, t)
    new_conv = new_conv_tb.reshape(tail, b, lru_w).transpose(1, 0, 2)
    if time_major:
        o_lru = o_lru_tb.reshape(t, b * lru_w)
    else:
        o_lru = o_lru_tb.reshape(t, b, lru_w).transpose(1, 0, 2).reshape(b * t, lru_w)

    qm0 = lru0 + 2 * lru_w
    mem_w = lw["w_br_mem"].shape[0]
    x1 = _merge(x2, xn, o_sb, o_lru, mem_k, mem_v, lw["w_in"][:, qm0:qm0 + mem_w], row(lw["q_norm_g"]),
                lw["w_in"][:, qm0 + mem_w:], row(lw["b_merge"]), lw["w_br_sb"], lw["w_br_lru"], lw["w_br_mem"],
                lw["w_out"], t, time_major)
    y = _ffn(x1, row(lw["norm_ffn_g"]), lw["w_ffn_gate"], lw["w_ffn_up"], lw["w_ffn_down"])
    return y.reshape(b, t, d), k_new, v_new, new_conv, h_last


_BF16_WEIGHTS = ("w_in", "lru_wa", "lru_wx", "w_br_sb", "w_br_lru", "w_br_mem", "w_out", "w_ffn_gate",
                 "w_ffn_up", "w_ffn_down")


def kernel(x_prompt, x_sample, mem_prompt, cache_sb_k, cache_sb_v, state_conv, state_lru_h, cache_mem_k, cache_mem_v, norm_mix_g, w_in, b_merge, conv_w, conv_b, lru_wa, lru_ba, lru_wx, lru_bx, lru_a_logit, q_norm_g, k_norm_g, mem_norm_g, w_mem_k, w_mem_v, w_br_sb, w_br_lru, w_br_mem, w_out, norm_ffn_g, w_ffn_gate, w_ffn_up, w_ffn_down):
    depth = w_in.shape[0]
    bp = x_prompt.shape[0]
    lru_w = conv_w.shape[2]
    hp, hs = x_prompt, x_sample
    outs = [[] for _ in range(10)]
    for l in range(depth):
        lw = {
            "norm_mix_g": norm_mix_g[l], "w_in": w_in[l], "b_merge": b_merge[l], "conv_w": conv_w[l],
            "conv_b": conv_b[l], "lru_wa": lru_wa[l], "lru_ba": lru_ba[l], "lru_wx": lru_wx[l],
            "lru_bx": lru_bx[l], "lru_a_logit": lru_a_logit[l], "q_norm_g": q_norm_g[l],
            "w_br_sb": w_br_sb[l], "w_br_lru": w_br_lru[l], "w_br_mem": w_br_mem[l], "w_out": w_out[l],
            "norm_ffn_g": norm_ffn_g[l], "w_ffn_gate": w_ffn_gate[l], "w_ffn_up": w_ffn_up[l],
            "w_ffn_down": w_ffn_down[l],
        }
        for name in _BF16_WEIGHTS:
            lw[name] = lw[name].astype(BF16)
        mk_p, mv_p = _memory_kv(mem_prompt, mem_norm_g[l].reshape(1, -1), w_mem_k[l].astype(BF16),
                                w_mem_v[l].astype(BF16), k_norm_g[l].reshape(1, -1))
        hp, k_p, v_p, c_p, s_p = _layer(hp, None, None, jnp.zeros((bp, CONV_WIDTH - 1, lru_w), F32),
                                        jnp.zeros((bp, lru_w), F32), mk_p, mv_p, lw)
        hs, k_s, v_s, c_s, s_s = _layer(hs, cache_sb_k[l], cache_sb_v[l], state_conv[l], state_lru_h[l],
                                        cache_mem_k[l], cache_mem_v[l], lw)
        for lst, val in zip(outs, (k_p, v_p, c_p, s_p, mk_p, mv_p, k_s, v_s, c_s, s_s)):
            lst.append(val)
    return (hp, hs) + tuple(lst[0][None] if depth == 1 else jnp.stack(lst) for lst in outs)
```

```python
import functools
import math

import jax
import jax.numpy as jnp
from jax import lax
from jax.experimental import pallas as pl
from jax.experimental.pallas import tpu as pltpu

F32 = jnp.float32
BF16 = jnp.bfloat16

EPS = 1e-6
LOG2E = 1.4426950408889634
SB_HEADS = 8
SB_HEAD_DIM = 128
LRU_BLOCKS = 8
LRU_C = 8.0
CONV_WIDTH = 4
MEM_HEADS = 4
N_BRANCH = 3

V7X_SUBLANES = 8
V7X_LANES = 128
V7X_MXU_DIM = 256
V7X_VMEM_LIMIT_BYTES = 56 * 1024 * 1024

SB_TILE = V7X_MXU_DIM
SB_HEADS_PER_STEP = 4
SB_EXIT_DROP = 150.0
TOKEN_TILE = 512
MERGE_SHORT_SEQUENCES = 4


def _dot(a, b):
    return jnp.dot(a, b, preferred_element_type=F32)


def _dot_nt(a, b):
    return lax.dot_general(a, b, (((1,), (1,)), ((), ())), preferred_element_type=F32)


def _sigmoid(x):
    return 1.0 / (1.0 + jnp.exp(-x))


def _rms(x, g):
    ms = jnp.mean(x * x, axis=-1, keepdims=True)
    return x * lax.rsqrt(ms + EPS) * g


def _const_spec(shape):
    zeros = (0,) * len(shape)
    return pl.BlockSpec(shape, lambda *_: zeros, pipeline_mode=pl.Buffered(1))


def _params(semantics):
    return pltpu.CompilerParams(dimension_semantics=semantics, vmem_limit_bytes=V7X_VMEM_LIMIT_BYTES)


def _qkv_kernel(x_ref, g_ref, w_ref, xn_ref, q_ref, kb_ref, vb_ref, kf_ref, vf_ref):
    xn = _rms(x_ref[...], g_ref[...]).astype(BF16)
    xn_ref[...] = xn
    bb, nh, tt, hd = q_ref.shape
    width = nh * hd
    q_scale = LOG2E / math.sqrt(hd)
    for part, (b_ref, f_ref) in enumerate(((q_ref, None), (kb_ref, kf_ref), (vb_ref, vf_ref))):
        res = _dot(xn, w_ref[:, part * width:(part + 1) * width])
        for h in range(nh):
            blk = res[:, h * hd:(h + 1) * hd].reshape(bb, tt, hd)
            if f_ref is None:
                blk = blk * q_scale
            else:
                f_ref[:, h] = blk
            b_ref[:, h] = blk.astype(BF16)


def _qkv_proj(x2, g, w_qkv, batch, seq):
    m, d = x2.shape
    nh, hd = SB_HEADS, SB_HEAD_DIM
    tm = min(TOKEN_TILE, m)
    if seq >= tm:
        bb, tt = 1, tm
    else:
        bb, tt = tm // seq, seq
    nt = seq // tt
    tok = pl.BlockSpec((tm, d), lambda i: (i, 0))
    head_spec = pl.BlockSpec((bb, nh, tt, hd), lambda i: (i // nt, 0, i % nt, 0))
    head_bf = jax.ShapeDtypeStruct((batch, nh, seq, hd), BF16)
    head_f32 = jax.ShapeDtypeStruct((batch, nh, seq, hd), F32)
    return pl.pallas_call(
        _qkv_kernel,
        grid=(m // tm,),
        in_specs=[tok, _const_spec((1, d)), _const_spec(w_qkv.shape)],
        out_specs=[tok, head_spec, head_spec, head_spec, head_spec, head_spec],
        out_shape=[jax.ShapeDtypeStruct((m, d), BF16), head_bf, head_bf, head_bf, head_f32, head_f32],
        compiler_params=_params(("parallel",)),
        name="qkv_proj",
    )(x2, g, w_qkv)


def _sb_scores(q, k):
    z = _dot_nt(q, k)
    neg_abs = lax.bitcast_convert_type(lax.bitcast_convert_type(z, jnp.uint32) | jnp.uint32(0x80000000), F32)
    softplus = jnp.maximum(z, 0.0) + jnp.log2(1.0 + jnp.exp2(neg_abs))
    return z - softplus, softplus


def _suffix_sums(drop, u):
    return _dot(drop.astype(BF16), u)


def _row_sum(x):
    return jnp.sum(x, axis=-1, keepdims=True)


def _sb_tile(q, k, v, u, c, mask):
    log_beta, drop = _sb_scores(q, k)
    if mask is not None:
        drop = jnp.where(mask, drop, 0.0)
    w = jnp.exp2(log_beta - _suffix_sums(drop, u) - c)
    if mask is not None:
        w = jnp.where(mask, w, 0.0)
    return _dot(w.astype(BF16), v), c + _row_sum(drop)


def _sb_own_and_previous(q, k2, v2, u, mask):
    tk = mask.shape[1]
    log_beta, drop = _sb_scores(q, k2)
    drop_own = jnp.where(mask, drop[:, tk:], 0.0)
    drop_prev = drop[:, :tk]
    c_own = _row_sum(drop_own)
    w_own = jnp.where(mask, jnp.exp2(log_beta[:, tk:] - _suffix_sums(drop_own, u)), 0.0)
    w_prev = jnp.exp2(log_beta[:, :tk] - _suffix_sums(drop_prev, u) - c_own)
    w = jnp.concatenate([w_prev, w_own], axis=1).astype(BF16)
    return _dot(w, v2), c_own + _row_sum(drop_prev)


def _causal_mask(tq, tk):
    row = lax.broadcasted_iota(jnp.int32, (tq, tk), 0)
    col = lax.broadcasted_iota(jnp.int32, (tq, tk), 1)
    return col < row


def _min_all(cs):
    return functools.reduce(jnp.minimum, [jnp.min(c) for c in cs])


def _sb_sweep_rest(first_tile, tile_fn, accs, cs):
    def cond(state):
        return jnp.logical_and(state[0] >= 0, state[1] < SB_EXIT_DROP)

    def body(state):
        kb, _, accs, cs = state
        new = [tile_fn(h, kb, cs[h]) for h in range(len(cs))]
        accs = tuple(acc + n[0] for acc, n in zip(accs, new))
        cs = tuple(n[1] for n in new)
        return kb - 1, _min_all(cs), accs, cs

    state = lax.while_loop(cond, body, (first_tile, _min_all(cs), tuple(accs), tuple(cs)))
    return state[2]


def _sb_prompt_kernel(q_ref, k_ref, v_ref, u_ref, o_ref):
    nh, tq, hd = q_ref.shape
    tk = tq
    qi = pl.program_id(2)
    u = u_ref[...]
    mask = _causal_mask(tq, tk)
    qs = [q_ref[h] for h in range(nh)]

    def own_only():
        outs = [_sb_tile(qs[h], k_ref[h, 0:tk, :], v_ref[h, 0:tk, :], u, 0.0, mask) for h in range(nh)]
        return tuple(o[0] for o in outs), tuple(o[1] for o in outs)

    def own_and_previous():
        start = pl.multiple_of((qi - 1) * tk, tk)
        outs = [_sb_own_and_previous(qs[h], k_ref[h, pl.ds(start, 2 * tk), :],
                                     v_ref[h, pl.ds(start, 2 * tk), :], u, mask) for h in range(nh)]
        return tuple(o[0] for o in outs), tuple(o[1] for o in outs)

    accs, cs = lax.cond(qi == 0, own_only, own_and_previous)

    def tile_fn(h, kb, c):
        start = pl.multiple_of(kb * tk, tk)
        return _sb_tile(qs[h], k_ref[h, pl.ds(start, tk), :], v_ref[h, pl.ds(start, tk), :], u, c, None)

    accs = _sb_sweep_rest(qi - 2, tile_fn, accs, cs)
    for h in range(nh):
        o_ref[:, h * hd:(h + 1) * hd] = accs[h].astype(o_ref.dtype)


def _sb_prompt(q, k, v, u):
    b, nh, t, hd = q.shape
    tq = u.shape[1]
    nq = t // tq
    hp = SB_HEADS_PER_STEP
    kv_spec = pl.BlockSpec((None, hp, t, hd), lambda bi, hg, qi: (bi, hg, 0, 0))
    return pl.pallas_call(
        _sb_prompt_kernel,
        grid=(b, nh // hp, nq),
        in_specs=[pl.BlockSpec((None, hp, tq, hd), lambda bi, hg, qi: (bi, hg, qi, 0)), kv_spec, kv_spec,
                  _const_spec(u.shape)],
        out_specs=pl.BlockSpec((tq, hp * hd), lambda bi, hg, qi: (bi * nq + qi, hg)),
        out_shape=jax.ShapeDtypeStruct((b * t, nh * hd), BF16),
        compiler_params=_params(("parallel", "parallel", "arbitrary")),
        name="sb_prompt",
    )(q, k, v, u)


def _sb_sample_kernel(q_ref, kn_ref, vn_ref, kp_ref, vp_ref, u_ref, us_ref, o_ref):
    nh, t, hd = q_ref.shape
    tk = u_ref.shape[1]
    n_past = kp_ref.shape[1] // tk
    u = u_ref[...]
    us = us_ref[...]
    mask = _causal_mask(t, t)
    qs = [q_ref[h] for h in range(nh)]

    def tile_fn(h, kb, c):
        start = pl.multiple_of(kb * tk, tk)
        k = kp_ref[h, pl.ds(start, tk), :].astype(BF16)
        v = vp_ref[h, pl.ds(start, tk), :].astype(BF16)
        return _sb_tile(qs[h], k, v, u, c, None)

    accs, cs = [], []
    for h in range(nh):
        acc_new, c = _sb_tile(qs[h], kn_ref[h], vn_ref[h], us, 0.0, mask)
        acc_past, c = tile_fn(h, n_past - 1, c)
        accs.append(acc_new + acc_past)
        cs.append(c)
    accs = _sb_sweep_rest(n_past - 2, tile_fn, accs, cs)
    for h in range(nh):
        o_ref[:, h * hd:(h + 1) * hd] = accs[h].astype(o_ref.dtype)


def _sb_sample(q, k_new, v_new, k_past, v_past, u, u_small):
    b, nh, t, hd = q.shape
    past = k_past.shape[2]
    new_spec = pl.BlockSpec((None, nh, t, hd), lambda bi: (bi, 0, 0, 0))
    past_spec = pl.BlockSpec((None, nh, past, hd), lambda bi: (bi, 0, 0, 0))
    return pl.pallas_call(
        _sb_sample_kernel,
        grid=(b,),
        in_specs=[new_spec, new_spec, new_spec, past_spec, past_spec, _const_spec(u.shape),
                  _const_spec(u_small.shape)],
        out_specs=pl.BlockSpec((t, nh * hd), lambda bi: (bi, 0)),
        out_shape=jax.ShapeDtypeStruct((b * t, nh * hd), BF16),
        compiler_params=_params(("parallel",)),
        name="sb_sample",
    )(q, k_new, v_new, k_past, v_past, u, u_small)


def _suffix_sum_matrix(n):
    r = lax.broadcasted_iota(jnp.int32, (n, n), 0)
    c = lax.broadcasted_iota(jnp.int32, (n, n), 1)
    return (r > c).astype(BF16)


def _lru_kernel(xn_ref, w_ref, cw_ref, cb_ref, wg_ref, ba_ref, bx_ref, al_ref, cs_ref, h0_ref,
                o_ref, nc_ref, hl_ref, xt, ht, tailbuf, hcar):
    batch, steps, d = xn_ref.shape
    width = o_ref.shape[2]
    rows = batch * steps
    slabs, _, lanes = xt.shape
    pitch = xt.shape[1] // batch
    tail_rows = (CONV_WIDTH - 1) * batch
    i = pl.program_id(0)
    last = pl.num_programs(0) - 1

    @pl.when(i == 0)
    def _():
        tailbuf[...] = cs_ref[...]
        hcar[...] = h0_ref[...]

    xn = xn_ref[...].reshape(rows, d)
    xl = _dot(xn, w_ref[:, :width])
    gl = _dot(xn, w_ref[:, width:])

    for j in range(slabs):
        for b in range(batch):
            xt[j, b * pitch:b * pitch + steps, :] = xl[b * steps:(b + 1) * steps, j * lanes:(j + 1) * lanes]

    def gather_step(ref, s):
        return jnp.concatenate(
            [jnp.concatenate([ref[j, pl.ds(g * V7X_SUBLANES * pitch + s, V7X_SUBLANES, stride=pitch), :]
                              for g in range(batch // V7X_SUBLANES)], axis=0) for j in range(slabs)], axis=1)

    xl_tm = jnp.concatenate([tailbuf[...]] + [gather_step(xt, s) for s in range(steps)], axis=0)
    tailbuf[...] = xl_tm[rows:rows + tail_rows, :]
    xc = cb_ref[...]
    for tap in range(CONV_WIDTH):
        xc = xc + xl_tm[tap * batch:tap * batch + rows, :] * cw_ref[tap:tap + 1, :]

    @pl.when(i == last)
    def _():
        nc_ref[...] = xl_tm[rows:rows + tail_rows, :]

    nb, bd, _ = wg_ref.shape
    ra, rx = [], []
    for n in range(nb):
        gates = _dot(xc[:, n * bd:(n + 1) * bd].astype(BF16), wg_ref[n])
        ra.append(gates[:, :bd])
        rx.append(gates[:, bd:])
    r = _sigmoid(jnp.concatenate(ra, axis=1) + ba_ref[...])
    ig = _sigmoid(jnp.concatenate(rx, axis=1) + bx_ref[...])

    al = al_ref[...]
    log_sig = jnp.minimum(al, 0.0) - jnp.log(1.0 + jnp.exp(-jnp.abs(al)))
    log_a = r * (LRU_C * log_sig)
    a = jnp.exp(log_a)
    th = jnp.tanh(log_a)
    m2 = -2.0 * th / (1.0 - th)
    mult = jnp.where(m2 > 0.0, m2 * lax.rsqrt(m2), 0.0)
    bterm = mult * (ig * xc)

    h = hcar[...]
    for s in range(steps):
        h = a[s * batch:(s + 1) * batch] * h + bterm[s * batch:(s + 1) * batch]
        for j in range(slabs):
            for g in range(batch // V7X_SUBLANES):
                ht[j, pl.ds(g * V7X_SUBLANES * pitch + s, V7X_SUBLANES, stride=pitch), :] = (
                    h[g * V7X_SUBLANES:(g + 1) * V7X_SUBLANES, j * lanes:(j + 1) * lanes])
    hcar[...] = h

    @pl.when(i == last)
    def _():
        hl_ref[...] = h

    hs = jnp.concatenate(
        [jnp.concatenate([ht[j, b * pitch:b * pitch + steps, :] for j in range(slabs)], axis=1)
         for b in range(batch)], axis=0)
    k1 = -2.0 * 0.7978845608028654 * LOG2E
    gate = gl / (1.0 + jnp.exp2(gl * (k1 + (k1 * 0.044715) * (gl * gl))))
    o_ref[...] = (hs * gate).astype(o_ref.dtype).reshape(batch, steps, width)


def _lru_branch(xn, w_lru, conv_w, conv_b, w_gates, ba, bx, a_logit, conv_state_tm, h0, batch, seq):
    m, d = xn.shape
    width = w_lru.shape[1] // 2
    tail_rows = (CONV_WIDTH - 1) * batch
    assert batch % V7X_SUBLANES == 0
    steps = max(1, min(seq, TOKEN_TILE // batch))
    rows = steps * batch
    pitch = steps + V7X_SUBLANES if (steps // V7X_SUBLANES) % 2 == 0 else steps
    slab = (width // V7X_LANES, batch * pitch, V7X_LANES)
    o_lru, new_conv_tm, h_last = pl.pallas_call(
        _lru_kernel,
        grid=(seq // steps,),
        in_specs=[pl.BlockSpec((batch, steps, d), lambda i: (0, i, 0)), _const_spec(w_lru.shape),
                  _const_spec(conv_w.shape), _const_spec((1, width)), _const_spec(w_gates.shape),
                  _const_spec((1, width)), _const_spec((1, width)), _const_spec((1, width)),
                  _const_spec((tail_rows, width)), _const_spec((batch, width))],
        out_specs=[pl.BlockSpec((batch, steps, width), lambda i: (0, i, 0)),
                   pl.BlockSpec((tail_rows, width), lambda i: (0, 0)),
                   pl.BlockSpec((batch, width), lambda i: (0, 0))],
        out_shape=[jax.ShapeDtypeStruct((batch, seq, width), BF16),
                   jax.ShapeDtypeStruct((tail_rows, width), F32),
                   jax.ShapeDtypeStruct((batch, width), F32)],
        scratch_shapes=[pltpu.VMEM(slab, F32), pltpu.VMEM(slab, F32), pltpu.VMEM((tail_rows, width), F32),
                        pltpu.VMEM((batch, width), F32)],
        compiler_params=_params(("arbitrary",)),
        name="lru_branch",
    )(xn.reshape(batch, seq, d), w_lru, conv_w, conv_b, w_gates, ba, bx, a_logit, conv_state_tm, h0)
    return o_lru.reshape(m, width), new_conv_tm, h_last


def _memkv_kernel(mem_ref, g_ref, wk_ref, wv_ref, kg_ref, k_ref, v_ref):
    nh, n, hd = k_ref.shape
    m = _rms(mem_ref[...], g_ref[...]).astype(BF16)
    k = _dot(m, wk_ref[...])
    v = _dot(m, wv_ref[...])
    for h in range(nh):
        k_ref[h] = _rms(k[:, h * hd:(h + 1) * hd], kg_ref[...])
        v_ref[h] = v[:, h * hd:(h + 1) * hd]


def _memory_kv(mem, g, wk, wv, kg):
    b, n, d = mem.shape
    nh = MEM_HEADS
    hd = wk.shape[1] // nh
    out = jax.ShapeDtypeStruct((b, nh, n, hd), F32)
    out_spec = pl.BlockSpec((None, nh, n, hd), lambda i: (i, 0, 0, 0))
    return pl.pallas_call(
        _memkv_kernel,
        grid=(b,),
        in_specs=[pl.BlockSpec((None, n, d), lambda i: (i, 0, 0)), _const_spec((1, d)), _const_spec(wk.shape),
                  _const_spec(wv.shape), _const_spec((1, hd))],
        out_specs=[out_spec, out_spec],
        out_shape=[out, out],
        compiler_params=_params(("parallel",)),
        name="memory_kv",
    )(mem, g, wk, wv, kg)


def _merge_kernel(x_ref, xn_ref, osb_ref, olru_ref, mk_ref, mv_ref, wqm_ref, qg_ref, wg_ref, bm_ref,
                  wsb_ref, wlru_ref, wmem_ref, wout_ref, o_ref):
    nb, nh, n_mem, hd = mk_ref.shape
    tm, d = o_ref.shape
    per_seq = tm // nb
    xn = xn_ref[...]

    qm = _dot(xn, wqm_ref[...])
    seqs = []
    for j in range(nb):
        heads = []
        for h in range(nh):
            qh = _rms(qm[j * per_seq:(j + 1) * per_seq, h * hd:(h + 1) * hd], qg_ref[...]).astype(BF16)
            s = _dot_nt(qh, mk_ref[j, h].astype(BF16)) / math.sqrt(hd)
            e = jnp.exp(s - jnp.max(s, axis=-1, keepdims=True))
            p = e / jnp.sum(e, axis=-1, keepdims=True)
            heads.append(_dot(p.astype(BF16), mv_ref[j, h].astype(BF16)))
        seqs.append(jnp.concatenate(heads, axis=1))
    omem = jnp.concatenate(seqs, axis=0).astype(BF16)

    gates = _sigmoid(_dot(xn, wg_ref[...]) + bm_ref[...])
    merged = (gates[:, :d] * _dot(osb_ref[...], wsb_ref[...])
              + gates[:, d:2 * d] * _dot(olru_ref[...], wlru_ref[...])
              + gates[:, 2 * d:] * _dot(omem, wmem_ref[...]))
    o_ref[...] = x_ref[...] + _dot(merged.astype(BF16), wout_ref[...])


def _merge(x2, xn, osb, olru, mk, mv, wqm, qg, wg, bm, wsb, wlru, wmem, wout, seq):
    m, d = x2.shape
    _, nh, n_mem, hd = mk.shape
    if seq >= TOKEN_TILE:
        tm, nb = TOKEN_TILE, 1
        per_batch = seq // tm
        mem_spec = pl.BlockSpec((nb, nh, n_mem, hd), lambda i: (i // per_batch, 0, 0, 0))
    else:
        nb = MERGE_SHORT_SEQUENCES
        tm = nb * seq
        mem_spec = pl.BlockSpec((nb, nh, n_mem, hd), lambda i: (i, 0, 0, 0))
    tok_of = lambda width: pl.BlockSpec((tm, width), lambda i: (i, 0))
    return pl.pallas_call(
        _merge_kernel,
        grid=(m // tm,),
        in_specs=[tok_of(d), tok_of(d), tok_of(osb.shape[1]), tok_of(olru.shape[1]), mem_spec, mem_spec,
                  _const_spec(wqm.shape), _const_spec((1, hd)), _const_spec(wg.shape),
                  _const_spec((1, N_BRANCH * d)), _const_spec(wsb.shape), _const_spec(wlru.shape),
                  _const_spec(wmem.shape), _const_spec(wout.shape)],
        out_specs=tok_of(d),
        out_shape=jax.ShapeDtypeStruct((m, d), F32),
        compiler_params=_params(("parallel",)),
        name="merge",
    )(x2, xn, osb, olru, mk, mv, wqm, qg, wg, bm, wsb, wlru, wmem, wout)


def _ffn_kernel(x_ref, g_ref, wgate_ref, wup_ref, wdown_ref, o_ref):
    x = x_ref[...]
    xn = _rms(x, g_ref[...]).astype(BF16)
    gate = _dot(xn, wgate_ref[...])
    up = _dot(xn, wup_ref[...])
    hidden = (gate * _sigmoid(gate) * up).astype(BF16)
    o_ref[...] = x + _dot(hidden, wdown_ref[...])


def _ffn(x1, g, wgate, wup, wdown):
    m, d = x1.shape
    tm = min(TOKEN_TILE, m)
    tok = pl.BlockSpec((tm, d), lambda i: (i, 0))
    return pl.pallas_call(
        _ffn_kernel,
        grid=(m // tm,),
        in_specs=[tok, _const_spec((1, d)), _const_spec(wgate.shape), _const_spec(wup.shape),
                  _const_spec(wdown.shape)],
        out_specs=tok,
        out_shape=jax.ShapeDtypeStruct((m, d), F32),
        compiler_params=_params(("parallel",)),
        name="ffn",
    )(x1, g, wgate, wup, wdown)


def _layer(x, past_k, past_v, conv_state, h0, mem_k, mem_v, lw):
    b, t, d = x.shape
    x2 = x.reshape(b * t, d)
    sbw = SB_HEADS * SB_HEAD_DIM
    lru_w = lw["conv_w"].shape[1]
    tail = CONV_WIDTH - 1
    row = lambda v: v.reshape(1, -1)

    xn, q, k_bf, v_bf, k_new, v_new = _qkv_proj(x2, row(lw["norm_mix_g"]), lw["w_in"][:, :3 * sbw], b, t)

    if past_k is None:
        o_sb = _sb_prompt(q, k_bf, v_bf, _suffix_sum_matrix(min(SB_TILE, t)))
    else:
        o_sb = _sb_sample(q, k_bf, v_bf, past_k, past_v, _suffix_sum_matrix(SB_TILE), _suffix_sum_matrix(t))

    lru0 = 3 * sbw
    w_gates = jnp.concatenate([lw["lru_wa"], lw["lru_wx"]], axis=2)
    o_lru, new_conv_tm, h_last = _lru_branch(
        xn, lw["w_in"][:, lru0:lru0 + 2 * lru_w], lw["conv_w"], row(lw["conv_b"]), w_gates,
        row(lw["lru_ba"]), row(lw["lru_bx"]), row(lw["lru_a_logit"]),
        conv_state.transpose(1, 0, 2).reshape(tail * b, lru_w), h0, b, t)
    new_conv = new_conv_tm.reshape(tail, b, lru_w).transpose(1, 0, 2)

    qm0 = lru0 + 2 * lru_w
    mem_w = lw["w_br_mem"].shape[0]
    x1 = _merge(x2, xn, o_sb, o_lru, mem_k, mem_v, lw["w_in"][:, qm0:qm0 + mem_w], row(lw["q_norm_g"]),
                lw["w_in"][:, qm0 + mem_w:], row(lw["b_merge"]), lw["w_br_sb"], lw["w_br_lru"], lw["w_br_mem"],
                lw["w_out"], t)
    y = _ffn(x1, row(lw["norm_ffn_g"]), lw["w_ffn_gate"], lw["w_ffn_up"], lw["w_ffn_down"])
    return y.reshape(b, t, d), k_new, v_new, new_conv, h_last


_BF16_WEIGHTS = ("w_in", "lru_wa", "lru_wx", "w_br_sb", "w_br_lru", "w_br_mem", "w_out", "w_ffn_gate",
                 "w_ffn_up", "w_ffn_down")


def kernel(x_prompt, x_sample, mem_prompt, cache_sb_k, cache_sb_v, state_conv, state_lru_h, cache_mem_k, cache_mem_v, norm_mix_g, w_in, b_merge, conv_w, conv_b, lru_wa, lru_ba, lru_wx, lru_bx, lru_a_logit, q_norm_g, k_norm_g, mem_norm_g, w_mem_k, w_mem_v, w_br_sb, w_br_lru, w_br_mem, w_out, norm_ffn_g, w_ffn_gate, w_ffn_up, w_ffn_down):
    depth = w_in.shape[0]
    bp = x_prompt.shape[0]
    lru_w = conv_w.shape[2]
    hp, hs = x_prompt, x_sample
    outs = [[] for _ in range(10)]
    for l in range(depth):
        lw = {
            "norm_mix_g": norm_mix_g[l], "w_in": w_in[l], "b_merge": b_merge[l], "conv_w": conv_w[l],
            "conv_b": conv_b[l], "lru_wa": lru_wa[l], "lru_ba": lru_ba[l], "lru_wx": lru_wx[l],
            "lru_bx": lru_bx[l], "lru_a_logit": lru_a_logit[l], "q_norm_g": q_norm_g[l],
            "w_br_sb": w_br_sb[l], "w_br_lru": w_br_lru[l], "w_br_mem": w_br_mem[l], "w_out": w_out[l],
            "norm_ffn_g": norm_ffn_g[l], "w_ffn_gate": w_ffn_gate[l], "w_ffn_up": w_ffn_up[l],
            "w_ffn_down": w_ffn_down[l],
        }
        for name in _BF16_WEIGHTS:
            lw[name] = lw[name].astype(BF16)
        mk_p, mv_p = _memory_kv(mem_prompt, mem_norm_g[l].reshape(1, -1), w_mem_k[l].astype(BF16),
                                w_mem_v[l].astype(BF16), k_norm_g[l].reshape(1, -1))
        hp, k_p, v_p, c_p, s_p = _layer(hp, None, None, jnp.zeros((bp, CONV_WIDTH - 1, lru_w), F32),
                                        jnp.zeros((bp, lru_w), F32), mk_p, mv_p, lw)
        hs, k_s, v_s, c_s, s_s = _layer(hs, cache_sb_k[l], cache_sb_v[l], state_conv[l], state_lru_h[l],
                                        cache_mem_k[l], cache_mem_v[l], lw)
        for lst, val in zip(outs, (k_p, v_p, c_p, s_p, mk_p, mv_p, k_s, v_s, c_s, s_s)):
            lst.append(val)
    return (hp, hs) + tuple(lst[0][None] if depth == 1 else jnp.stack(lst) for lst in outs)
```

```python
import functools
import math

import jax
import jax.numpy as jnp
from jax import lax
from jax.experimental import pallas as pl
from jax.experimental.pallas import tpu as pltpu

F32 = jnp.float32
BF16 = jnp.bfloat16

EPS = 1e-6
LOG2E = 1.4426950408889634
SB_HEADS = 8
SB_HEAD_DIM = 128
LRU_BLOCKS = 8
LRU_C = 8.0
CONV_WIDTH = 4
MEM_HEADS = 4
N_BRANCH = 3

V7X_SUBLANES = 8
V7X_LANES = 128
V7X_MXU_DIM = 256
V7X_VMEM_LIMIT_BYTES = 56 * 1024 * 1024

SB_TILE = V7X_MXU_DIM
SB_HEADS_PER_STEP = 4
SB_EXIT_DROP = 150.0
TOKEN_TILE = 512
MERGE_SHORT_SEQUENCES = 4


def _dot(a, b):
    return jnp.dot(a, b, preferred_element_type=F32)


def _dot_nt(a, b):
    return lax.dot_general(a, b, (((1,), (1,)), ((), ())), preferred_element_type=F32)


def _sigmoid(x):
    return 1.0 / (1.0 + jnp.exp(-x))


def _rms(x, g):
    ms = jnp.mean(x * x, axis=-1, keepdims=True)
    return x * lax.rsqrt(ms + EPS) * g


def _const_spec(shape):
    zeros = (0,) * len(shape)
    return pl.BlockSpec(shape, lambda *_: zeros, pipeline_mode=pl.Buffered(1))


def _params(semantics):
    return pltpu.CompilerParams(dimension_semantics=semantics, vmem_limit_bytes=V7X_VMEM_LIMIT_BYTES)


def _sb_scores(q, k):
    z = _dot_nt(q, k)
    neg, pos = jnp.minimum(z, 0.0), jnp.maximum(z, 0.0)
    tail = jnp.log2(1.0 + jnp.exp2(neg - pos))
    return neg - tail, pos + tail


def _suffix_sums(drop, u):
    return _dot(drop.astype(BF16), u)


def _row_sum(x):
    return jnp.sum(x, axis=-1, keepdims=True)


def _sb_tile(q, k, v, u, c, mask):
    log_beta, drop = _sb_scores(q, k)
    if mask is not None:
        drop = jnp.where(mask, drop, 0.0)
    w = jnp.exp2(log_beta - _suffix_sums(drop, u) - c)
    if mask is not None:
        w = jnp.where(mask, w, 0.0)
    return _dot(w.astype(BF16), v), c + _row_sum(drop)


def _sb_own_and_previous(qs, k2s, v2s, u, mask):
    tk = mask.shape[1]
    scores = [_sb_scores(q, k2) for q, k2 in zip(qs, k2s)]
    drops = [(jnp.where(mask, drop[:, tk:], 0.0), drop[:, :tk]) for _, drop in scores]
    sums = [(_suffix_sums(own, u), _suffix_sums(prev, u)) for own, prev in drops]
    accs, cs = [], []
    for (log_beta, _), (drop_own, drop_prev), (sum_own, sum_prev), v2 in zip(scores, drops, sums, v2s):
        c_own = _row_sum(drop_own)
        w_own = jnp.where(mask, jnp.exp2(log_beta[:, tk:] - sum_own), 0.0)
        w_prev = jnp.exp2(log_beta[:, :tk] - sum_prev - c_own)
        w = jnp.concatenate([w_prev, w_own], axis=1).astype(BF16)
        accs.append(_dot(w, v2))
        cs.append(c_own + _row_sum(drop_prev))
    return tuple(accs), tuple(cs)


def _causal_mask(tq, tk):
    row = lax.broadcasted_iota(jnp.int32, (tq, tk), 0)
    col = lax.broadcasted_iota(jnp.int32, (tq, tk), 1)
    return col < row


def _min_all(cs):
    return functools.reduce(jnp.minimum, [jnp.min(c) for c in cs])


def _sb_sweep_rest(first_tile, tile_fn, accs, cs):
    def cond(state):
        return jnp.logical_and(state[0] >= 0, state[1] < SB_EXIT_DROP)

    def body(state):
        kb, _, accs, cs = state
        new = [tile_fn(h, kb, cs[h]) for h in range(len(cs))]
        accs = tuple(acc + n[0] for acc, n in zip(accs, new))
        cs = tuple(n[1] for n in new)
        return kb - 1, _min_all(cs), accs, cs

    state = lax.while_loop(cond, body, (first_tile, _min_all(cs), tuple(accs), tuple(cs)))
    return state[2]


def _sb_prompt_kernel(q_ref, k_ref, v_ref, u_ref, o_ref):
    nh, tq, hd = q_ref.shape
    tk = tq
    qi = pl.program_id(2)
    u = u_ref[...]
    mask = _causal_mask(tq, tk)
    qs = [q_ref[h] for h in range(nh)]

    def own_only():
        outs = [_sb_tile(qs[h], k_ref[h, 0:tk, :], v_ref[h, 0:tk, :], u, 0.0, mask) for h in range(nh)]
        return tuple(o[0] for o in outs), tuple(o[1] for o in outs)

    def own_and_previous():
        start = pl.multiple_of((qi - 1) * tk, tk)
        return _sb_own_and_previous(qs, [k_ref[h, pl.ds(start, 2 * tk), :] for h in range(nh)],
                                    [v_ref[h, pl.ds(start, 2 * tk), :] for h in range(nh)], u, mask)

    accs, cs = lax.cond(qi == 0, own_only, own_and_previous)

    def tile_fn(h, kb, c):
        start = pl.multiple_of(kb * tk, tk)
        return _sb_tile(qs[h], k_ref[h, pl.ds(start, tk), :], v_ref[h, pl.ds(start, tk), :], u, c, None)

    accs = _sb_sweep_rest(qi - 2, tile_fn, accs, cs)
    for h in range(nh):
        o_ref[:, h * hd:(h + 1) * hd] = accs[h].astype(o_ref.dtype)


def _sb_prompt(q, k, v, u):
    b, nh, t, hd = q.shape
    tq = u.shape[1]
    nq = t // tq
    hp = SB_HEADS_PER_STEP
    kv_spec = pl.BlockSpec((None, hp, t, hd), lambda bi, hg, qi: (bi, hg, 0, 0))
    return pl.pallas_call(
        _sb_prompt_kernel,
        grid=(b, nh // hp, nq),
        in_specs=[pl.BlockSpec((None, hp, tq, hd), lambda bi, hg, qi: (bi, hg, qi, 0)), kv_spec, kv_spec,
                  _const_spec(u.shape)],
        out_specs=pl.BlockSpec((tq, hp * hd), lambda bi, hg, qi: (bi * nq + qi, hg)),
        out_shape=jax.ShapeDtypeStruct((b * t, nh * hd), BF16),
        compiler_params=_params(("parallel", "parallel", "arbitrary")),
        name="sb_prompt",
    )(q, k, v, u)


def _sb_sample_kernel(q_ref, kn_ref, vn_ref, kp_ref, vp_ref, u_ref, us_ref, o_ref):
    nh, t, hd = q_ref.shape
    tk = u_ref.shape[1]
    n_past = kp_ref.shape[1] // tk
    u = u_ref[...]
    us = us_ref[...]
    mask = _causal_mask(t, t)
    qs = [q_ref[h] for h in range(nh)]

    def tile_fn(h, kb, c):
        start = pl.multiple_of(kb * tk, tk)
        k = kp_ref[h, pl.ds(start, tk), :].astype(BF16)
        v = vp_ref[h, pl.ds(start, tk), :].astype(BF16)
        return _sb_tile(qs[h], k, v, u, c, None)

    accs, cs = [], []
    for h in range(nh):
        acc_new, c = _sb_tile(qs[h], kn_ref[h], vn_ref[h], us, 0.0, mask)
        acc_past, c = tile_fn(h, n_past - 1, c)
        accs.append(acc_new + acc_past)
        cs.append(c)
    accs = _sb_sweep_rest(n_past - 2, tile_fn, accs, cs)
    for h in range(nh):
        o_ref[:, h * hd:(h + 1) * hd] = accs[h].astype(o_ref.dtype)


def _sb_sample(q, k_new, v_new, k_past, v_past, u, u_small):
    b, nh, t, hd = q.shape
    past = k_past.shape[2]
    new_spec = pl.BlockSpec((None, nh, t, hd), lambda bi: (bi, 0, 0, 0))
    past_spec = pl.BlockSpec((None, nh, past, hd), lambda bi: (bi, 0, 0, 0))
    return pl.pallas_call(
        _sb_sample_kernel,
        grid=(b,),
        in_specs=[new_spec, new_spec, new_spec, past_spec, past_spec, _const_spec(u.shape),
                  _const_spec(u_small.shape)],
        out_specs=pl.BlockSpec((t, nh * hd), lambda bi: (bi, 0)),
        out_shape=jax.ShapeDtypeStruct((b * t, nh * hd), BF16),
        compiler_params=_params(("parallel",)),
        name="sb_sample",
    )(q, k_new, v_new, k_past, v_past, u, u_small)


def _suffix_sum_matrix(n):
    r = lax.broadcasted_iota(jnp.int32, (n, n), 0)
    c = lax.broadcasted_iota(jnp.int32, (n, n), 1)
    return (r > c).astype(BF16)


def _in_kernel(x_ref, g_ref, wqkv_ref, w_ref, cw_ref, cb_ref, wg_ref, ba_ref, bx_ref, al_ref, cs_ref, h0_ref,
               xn_ref, q_ref, kb_ref, vb_ref, kf_ref, vf_ref, o_ref, nc_ref, hl_ref, xt, ht, tailbuf, hcar):
    batch, steps, d = x_ref.shape
    width = o_ref.shape[2]
    rows = batch * steps
    slabs, _, lanes = xt.shape
    pitch = xt.shape[1] // batch
    groups = batch // V7X_SUBLANES
    tail_rows = (CONV_WIDTH - 1) * batch
    i = pl.program_id(0)
    last = pl.num_programs(0) - 1

    @pl.when(i == 0)
    def _():
        tailbuf[...] = cs_ref[...]
        hcar[...] = h0_ref[...]

    xn = _rms(x_ref[...].reshape(rows, d), g_ref[...]).astype(BF16)
    xn_ref[...] = xn.reshape(batch, steps, d)
    xl = _dot(xn, w_ref[:, :width])
    for j in range(slabs):
        for b in range(batch):
            xt[j, b * pitch:b * pitch + steps, :] = xl[b * steps:(b + 1) * steps, j * lanes:(j + 1) * lanes]

    nh, hd = q_ref.shape[1], q_ref.shape[3]
    q_scale = LOG2E / math.sqrt(hd)
    half = width // 2
    gl_halves = []

    def project_gate_half(k):
        gl_halves.append(_dot(xn, w_ref[:, width + k * half:width + (k + 1) * half]))

    def project_heads(part, k, b_ref, f_ref):
        res = _dot(xn, wqkv_ref[:, part * nh * hd + k * half:part * nh * hd + (k + 1) * half])
        for h in range(half // hd):
            blk = res[:, h * hd:(h + 1) * hd].reshape(batch, steps, hd)
            if f_ref is None:
                blk = blk * q_scale
            else:
                f_ref[:, k * (half // hd) + h] = blk
            b_ref[:, k * (half // hd) + h] = blk.astype(BF16)

    filler = [functools.partial(project_gate_half, 0), functools.partial(project_gate_half, 1)]
    for part, (b_ref, f_ref) in enumerate(((q_ref, None), (kb_ref, kf_ref), (vb_ref, vf_ref))):
        filler += [functools.partial(project_heads, part, k, b_ref, f_ref) for k in range(2)]
    assert len(filler) == slabs

    al = al_ref[...]
    log_sig8 = LRU_C * (jnp.minimum(al, 0.0) - jnp.log(1.0 + jnp.exp(-jnp.abs(al))))
    k1 = -2.0 * 0.7978845608028654 * LOG2E

    assert wg_ref.shape[0] == slabs and wg_ref.shape[1] == lanes
    for j in range(slabs):
        cols = slice(j * lanes, (j + 1) * lanes)
        gathered = [jnp.concatenate([xt[j, pl.ds(g * V7X_SUBLANES * pitch + s, V7X_SUBLANES, stride=pitch), :]
                                     for g in range(groups)], axis=0) for s in range(steps)]
        xl_tm = jnp.concatenate([tailbuf[:, cols]] + gathered, axis=0)
        tailbuf[:, cols] = xl_tm[rows:rows + tail_rows, :]
        xc = cb_ref[:, cols]
        for tap in range(CONV_WIDTH):
            xc = xc + xl_tm[tap * batch:tap * batch + rows, :] * cw_ref[tap:tap + 1, cols]
        gates = _dot(xc.astype(BF16), wg_ref[j])
        filler[j]()

        r = _sigmoid(gates[:, :lanes] + ba_ref[:, cols])
        ig = _sigmoid(gates[:, lanes:] + bx_ref[:, cols])
        log_a = r * log_sig8[:, cols]
        a = jnp.exp(log_a)
        th = jnp.tanh(log_a)
        m2 = -2.0 * th / (1.0 - th)
        bterm = jnp.where(m2 > 0.0, m2 * lax.rsqrt(m2), 0.0) * (ig * xc)

        h = hcar[:, cols]
        for s in range(steps):
            h = a[s * batch:(s + 1) * batch] * h + bterm[s * batch:(s + 1) * batch]
            for g in range(groups):
                ht[j, pl.ds(g * V7X_SUBLANES * pitch + s, V7X_SUBLANES, stride=pitch), :] = (
                    h[g * V7X_SUBLANES:(g + 1) * V7X_SUBLANES])
        hcar[:, cols] = h

        hs = jnp.concatenate([ht[j, b * pitch:b * pitch + steps, :] for b in range(batch)], axis=0)
        off = (j * lanes) % half
        glj = gl_halves[j * lanes // half][:, off:off + lanes]
        gate = glj / (1.0 + jnp.exp2(glj * (k1 + (k1 * 0.044715) * (glj * glj))))
        o_ref[:, :, cols] = (hs * gate).astype(o_ref.dtype).reshape(batch, steps, lanes)

    @pl.when(i == last)
    def _():
        nc_ref[...] = tailbuf[...]
        hl_ref[...] = hcar[...]


def _in_proj(x, g, w_qkv, w_lru, conv_w, conv_b, w_gates, ba, bx, a_logit, conv_state_tm, h0):
    batch, seq, d = x.shape
    nh, hd = SB_HEADS, SB_HEAD_DIM
    width = w_lru.shape[1] // 2
    tail_rows = (CONV_WIDTH - 1) * batch
    assert batch % V7X_SUBLANES == 0
    steps = max(1, min(seq, TOKEN_TILE // batch))
    pitch = steps + V7X_SUBLANES if (steps // V7X_SUBLANES) % 2 == 0 else steps
    slab = (width // V7X_LANES, batch * pitch, V7X_LANES)
    tok_of = lambda w: pl.BlockSpec((batch, steps, w), lambda i: (0, i, 0))
    head_spec = pl.BlockSpec((batch, nh, steps, hd), lambda i: (0, 0, i, 0))
    head_bf = jax.ShapeDtypeStruct((batch, nh, seq, hd), BF16)
    head_f32 = jax.ShapeDtypeStruct((batch, nh, seq, hd), F32)
    return pl.pallas_call(
        _in_kernel,
        grid=(seq // steps,),
        in_specs=[tok_of(d), _const_spec((1, d)), _const_spec(w_qkv.shape), _const_spec(w_lru.shape),
                  _const_spec(conv_w.shape), _const_spec((1, width)), _const_spec(w_gates.shape),
                  _const_spec((1, width)), _const_spec((1, width)), _const_spec((1, width)),
                  _const_spec((tail_rows, width)), _const_spec((batch, width))],
        out_specs=[tok_of(d), head_spec, head_spec, head_spec, head_spec, head_spec, tok_of(width),
                   pl.BlockSpec((tail_rows, width), lambda i: (0, 0)),
                   pl.BlockSpec((batch, width), lambda i: (0, 0))],
        out_shape=[jax.ShapeDtypeStruct((batch, seq, d), BF16), head_bf, head_bf, head_bf, head_f32, head_f32,
                   jax.ShapeDtypeStruct((batch, seq, width), BF16),
                   jax.ShapeDtypeStruct((tail_rows, width), F32),
                   jax.ShapeDtypeStruct((batch, width), F32)],
        scratch_shapes=[pltpu.VMEM(slab, F32), pltpu.VMEM(slab, F32), pltpu.VMEM((tail_rows, width), F32),
                        pltpu.VMEM((batch, width), F32)],
        compiler_params=_params(("arbitrary",)),
        name="in_proj",
    )(x, g, w_qkv, w_lru, conv_w, conv_b, w_gates, ba, bx, a_logit, conv_state_tm, h0)


def _memkv_kernel(mem_ref, g_ref, wk_ref, wv_ref, kg_ref, k_ref, v_ref):
    nh, n, hd = k_ref.shape
    m = _rms(mem_ref[...], g_ref[...]).astype(BF16)
    k = _dot(m, wk_ref[...])
    v = _dot(m, wv_ref[...])
    for h in range(nh):
        k_ref[h] = _rms(k[:, h * hd:(h + 1) * hd], kg_ref[...])
        v_ref[h] = v[:, h * hd:(h + 1) * hd]


def _memory_kv(mem, g, wk, wv, kg):
    b, n, d = mem.shape
    nh = MEM_HEADS
    hd = wk.shape[1] // nh
    out = jax.ShapeDtypeStruct((b, nh, n, hd), F32)
    out_spec = pl.BlockSpec((None, nh, n, hd), lambda i: (i, 0, 0, 0))
    return pl.pallas_call(
        _memkv_kernel,
        grid=(b,),
        in_specs=[pl.BlockSpec((None, n, d), lambda i: (i, 0, 0)), _const_spec((1, d)), _const_spec(wk.shape),
                  _const_spec(wv.shape), _const_spec((1, hd))],
        out_specs=[out_spec, out_spec],
        out_shape=[out, out],
        compiler_params=_params(("parallel",)),
        name="memory_kv",
    )(mem, g, wk, wv, kg)


def _merge_kernel(x_ref, xn_ref, osb_ref, olru_ref, mk_ref, mv_ref, wqm_ref, qg_ref, wg_ref, bm_ref,
                  wsb_ref, wlru_ref, wmem_ref, wout_ref, o_ref):
    nb, nh, n_mem, hd = mk_ref.shape
    tm, d = o_ref.shape
    per_seq = tm // nb
    xn = xn_ref[...]

    qm = _dot(xn, wqm_ref[...])
    seqs = []
    for j in range(nb):
        heads = []
        for h in range(nh):
            qh = _rms(qm[j * per_seq:(j + 1) * per_seq, h * hd:(h + 1) * hd], qg_ref[...]).astype(BF16)
            s = _dot_nt(qh, mk_ref[j, h].astype(BF16)) / math.sqrt(hd)
            e = jnp.exp(s - jnp.max(s, axis=-1, keepdims=True))
            p = e / jnp.sum(e, axis=-1, keepdims=True)
            heads.append(_dot(p.astype(BF16), mv_ref[j, h].astype(BF16)))
        seqs.append(jnp.concatenate(heads, axis=1))
    omem = jnp.concatenate(seqs, axis=0).astype(BF16)

    gates = _sigmoid(_dot(xn, wg_ref[...]) + bm_ref[...])
    merged = (gates[:, :d] * _dot(osb_ref[...], wsb_ref[...])
              + gates[:, d:2 * d] * _dot(olru_ref[...], wlru_ref[...])
              + gates[:, 2 * d:] * _dot(omem, wmem_ref[...]))
    o_ref[...] = x_ref[...] + _dot(merged.astype(BF16), wout_ref[...])


def _merge(x2, xn, osb, olru, mk, mv, wqm, qg, wg, bm, wsb, wlru, wmem, wout, seq):
    m, d = x2.shape
    _, nh, n_mem, hd = mk.shape
    if seq >= TOKEN_TILE:
        tm, nb = TOKEN_TILE, 1
        per_batch = seq // tm
        mem_spec = pl.BlockSpec((nb, nh, n_mem, hd), lambda i: (i // per_batch, 0, 0, 0))
    else:
        nb = MERGE_SHORT_SEQUENCES
        tm = nb * seq
        mem_spec = pl.BlockSpec((nb, nh, n_mem, hd), lambda i: (i, 0, 0, 0))
    tok_of = lambda width: pl.BlockSpec((tm, width), lambda i: (i, 0))
    return pl.pallas_call(
        _merge_kernel,
        grid=(m // tm,),
        in_specs=[tok_of(d), tok_of(d), tok_of(osb.shape[1]), tok_of(olru.shape[1]), mem_spec, mem_spec,
                  _const_spec(wqm.shape), _const_spec((1, hd)), _const_spec(wg.shape),
                  _const_spec((1, N_BRANCH * d)), _const_spec(wsb.shape), _const_spec(wlru.shape),
                  _const_spec(wmem.shape), _const_spec(wout.shape)],
        out_specs=tok_of(d),
        out_shape=jax.ShapeDtypeStruct((m, d), F32),
        compiler_params=_params(("parallel",)),
        name="merge",
    )(x2, xn, osb, olru, mk, mv, wqm, qg, wg, bm, wsb, wlru, wmem, wout)


def _ffn_kernel(x_ref, g_ref, wgate_ref, wup_ref, wdown_ref, o_ref):
    x = x_ref[...]
    xn = _rms(x, g_ref[...]).astype(BF16)
    gate = _dot(xn, wgate_ref[...])
    up = _dot(xn, wup_ref[...])
    hidden = (gate * _sigmoid(gate) * up).astype(BF16)
    o_ref[...] = x + _dot(hidden, wdown_ref[...])


def _ffn(x1, g, wgate, wup, wdown):
    m, d = x1.shape
    tm = min(TOKEN_TILE, m)
    tok = pl.BlockSpec((tm, d), lambda i: (i, 0))
    return pl.pallas_call(
        _ffn_kernel,
        grid=(m // tm,),
        in_specs=[tok, _const_spec((1, d)), _const_spec(wgate.shape), _const_spec(wup.shape),
                  _const_spec(wdown.shape)],
        out_specs=tok,
        out_shape=jax.ShapeDtypeStruct((m, d), F32),
        compiler_params=_params(("parallel",)),
        name="ffn",
    )(x1, g, wgate, wup, wdown)


def _layer(x, past_k, past_v, conv_state, h0, mem_k, mem_v, lw):
    b, t, d = x.shape
    x2 = x.reshape(b * t, d)
    sbw = SB_HEADS * SB_HEAD_DIM
    lru_w = lw["conv_w"].shape[1]
    tail = CONV_WIDTH - 1
    row = lambda v: v.reshape(1, -1)

    lru0 = 3 * sbw
    w_gates = jnp.concatenate([lw["lru_wa"], lw["lru_wx"]], axis=2)
    xn, q, k_bf, v_bf, k_new, v_new, o_lru, new_conv_tm, h_last = _in_proj(
        x, row(lw["norm_mix_g"]), lw["w_in"][:, :lru0], lw["w_in"][:, lru0:lru0 + 2 * lru_w], lw["conv_w"],
        row(lw["conv_b"]), w_gates, row(lw["lru_ba"]), row(lw["lru_bx"]), row(lw["lru_a_logit"]),
        conv_state.transpose(1, 0, 2).reshape(tail * b, lru_w), h0)
    new_conv = new_conv_tm.reshape(tail, b, lru_w).transpose(1, 0, 2)

    if past_k is None:
        o_sb = _sb_prompt(q, k_bf, v_bf, _suffix_sum_matrix(min(SB_TILE, t)))
    else:
        o_sb = _sb_sample(q, k_bf, v_bf, past_k, past_v, _suffix_sum_matrix(SB_TILE), _suffix_sum_matrix(t))

    qm0 = lru0 + 2 * lru_w
    mem_w = lw["w_br_mem"].shape[0]
    x1 = _merge(x2, xn.reshape(b * t, d), o_sb, o_lru.reshape(b * t, lru_w), mem_k, mem_v,
                lw["w_in"][:, qm0:qm0 + mem_w], row(lw["q_norm_g"]),
                lw["w_in"][:, qm0 + mem_w:], row(lw["b_merge"]), lw["w_br_sb"], lw["w_br_lru"], lw["w_br_mem"],
                lw["w_out"], t)
    y = _ffn(x1, row(lw["norm_ffn_g"]), lw["w_ffn_gate"], lw["w_ffn_up"], lw["w_ffn_down"])
    return y.reshape(b, t, d), k_new, v_new, new_conv, h_last


_BF16_WEIGHTS = ("w_in", "lru_wa", "lru_wx", "w_br_sb", "w_br_lru", "w_br_mem", "w_out", "w_ffn_gate",
                 "w_ffn_up", "w_ffn_down")


def kernel(x_prompt, x_sample, mem_prompt, cache_sb_k, cache_sb_v, state_conv, state_lru_h, cache_mem_k, cache_mem_v, norm_mix_g, w_in, b_merge, conv_w, conv_b, lru_wa, lru_ba, lru_wx, lru_bx, lru_a_logit, q_norm_g, k_norm_g, mem_norm_g, w_mem_k, w_mem_v, w_br_sb, w_br_lru, w_br_mem, w_out, norm_ffn_g, w_ffn_gate, w_ffn_up, w_ffn_down):
    depth = w_in.shape[0]
    bp = x_prompt.shape[0]
    lru_w = conv_w.shape[2]
    hp, hs = x_prompt, x_sample
    outs = [[] for _ in range(10)]
    for l in range(depth):
        lw = {
            "norm_mix_g": norm_mix_g[l], "w_in": w_in[l], "b_merge": b_merge[l], "conv_w": conv_w[l],
            "conv_b": conv_b[l], "lru_wa": lru_wa[l], "lru_ba": lru_ba[l], "lru_wx": lru_wx[l],
            "lru_bx": lru_bx[l], "lru_a_logit": lru_a_logit[l], "q_norm_g": q_norm_g[l],
            "w_br_sb": w_br_sb[l], "w_br_lru": w_br_lru[l], "w_br_mem": w_br_mem[l], "w_out": w_out[l],
            "norm_ffn_g": norm_ffn_g[l], "w_ffn_gate": w_ffn_gate[l], "w_ffn_up": w_ffn_up[l],
            "w_ffn_down": w_ffn_down[l],
        }
        for name in _BF16_WEIGHTS:
            lw[name] = lw[name].astype(BF16)
        mk_p, mv_p = _memory_kv(mem_prompt, mem_norm_g[l].reshape(1, -1), w_mem_k[l].astype(BF16),
                                w_mem_v[l].astype(BF16), k_norm_g[l].reshape(1, -1))
        hp, k_p, v_p, c_p, s_p = _layer(hp, None, None, jnp.zeros((bp, CONV_WIDTH - 1, lru_w), F32),
                                        jnp.zeros((bp, lru_w), F32), mk_p, mv_p, lw)
        hs, k_s, v_s, c_s, s_s = _layer(hs, cache_sb_k[l], cache_sb_v[l], state_conv[l], state_lru_h[l],
                                        cache_mem_k[l], cache_mem_v[l], lw)
        for lst, val in zip(outs, (k_p, v_p, c_p, s_p, mk_p, mv_p, k_s, v_s, c_s, s_s)):
            lst.append(val)
    return (hp, hs) + tuple(lst[0][None] if depth == 1 else jnp.stack(lst) for lst in outs)
```

```python
import functools
import math

import jax
import jax.numpy as jnp
from jax import lax
from jax.experimental import pallas as pl
from jax.experimental.pallas import tpu as pltpu

F32 = jnp.float32
BF16 = jnp.bfloat16

EPS = 1e-6
LOG2E = 1.4426950408889634
SB_HEADS = 8
SB_HEAD_DIM = 128
LRU_BLOCKS = 8
LRU_C = 8.0
CONV_WIDTH = 4
MEM_HEADS = 4
N_BRANCH = 3

V7X_SUBLANES = 8
V7X_LANES = 128
V7X_MXU_DIM = 256
V7X_VMEM_LIMIT_BYTES = 56 * 1024 * 1024

SB_TILE = V7X_MXU_DIM
SB_HEADS_PER_STEP = 4
SB_QUERY_TILES_PER_STEP = 2
SB_EXIT_DROP = 150.0
TOKEN_TILE = 512
MERGE_SHORT_SEQUENCES = 4


def _dot(a, b):
    return jnp.dot(a, b, preferred_element_type=F32)


def _dot_nt(a, b):
    return lax.dot_general(a, b, (((1,), (1,)), ((), ())), preferred_element_type=F32)


def _sigmoid(x):
    return 1.0 / (1.0 + jnp.exp(-x))


def _rms(x, g):
    ms = jnp.mean(x * x, axis=-1, keepdims=True)
    return x * lax.rsqrt(ms + EPS) * g


def _const_spec(shape):
    zeros = (0,) * len(shape)
    return pl.BlockSpec(shape, lambda *_: zeros, pipeline_mode=pl.Buffered(1))


def _params(semantics):
    return pltpu.CompilerParams(dimension_semantics=semantics, vmem_limit_bytes=V7X_VMEM_LIMIT_BYTES)


def _sb_scores(q, k):
    z = _dot_nt(q, k)
    neg, pos = jnp.minimum(z, 0.0), jnp.maximum(z, 0.0)
    tail = jnp.log2(1.0 + jnp.exp2(neg - pos))
    return neg - tail, pos + tail


def _suffix_sums(drop, u):
    return _dot(drop.astype(BF16), u)


def _row_sum(x):
    return jnp.sum(x, axis=-1, keepdims=True)


def _sb_tiles(qs, ks, vs, u, cs, mask):
    scores = [_sb_scores(q, k) for q, k in zip(qs, ks)]
    drops = [drop if mask is None else jnp.where(mask, drop, 0.0) for _, drop in scores]
    sums = [_suffix_sums(drop, u) for drop in drops]
    accs, new_cs = [], []
    for (log_beta, _), drop, s, v, c in zip(scores, drops, sums, vs, cs):
        w = jnp.exp2(log_beta - s - c)
        if mask is not None:
            w = jnp.where(mask, w, 0.0)
        accs.append(_dot(w.astype(BF16), v))
        new_cs.append(c + _row_sum(drop))
    return tuple(accs), tuple(new_cs)


def _sb_own_and_previous(qs, k2s, v2s, u, mask):
    tk = mask.shape[1]
    scores = [_sb_scores(q, k2) for q, k2 in zip(qs, k2s)]
    drops = [(jnp.where(mask, drop[:, tk:], 0.0), drop[:, :tk]) for _, drop in scores]
    sums = [(_suffix_sums(own, u), _suffix_sums(prev, u)) for own, prev in drops]
    accs, cs = [], []
    for (log_beta, _), (drop_own, drop_prev), (sum_own, sum_prev), v2 in zip(scores, drops, sums, v2s):
        c_own = _row_sum(drop_own)
        w_own = jnp.where(mask, jnp.exp2(log_beta[:, tk:] - sum_own), 0.0)
        w_prev = jnp.exp2(log_beta[:, :tk] - sum_prev - c_own)
        w = jnp.concatenate([w_prev, w_own], axis=1).astype(BF16)
        accs.append(_dot(w, v2))
        cs.append(c_own + _row_sum(drop_prev))
    return tuple(accs), tuple(cs)


def _causal_mask(tq, tk):
    row = lax.broadcasted_iota(jnp.int32, (tq, tk), 0)
    col = lax.broadcasted_iota(jnp.int32, (tq, tk), 1)
    return col < row


def _min_all(cs):
    return functools.reduce(jnp.minimum, [jnp.min(c) for c in cs])


def _sb_sweep_rest(first_tile, tiles_fn, accs, cs):
    def cond(state):
        return jnp.logical_and(state[0] >= 0, state[1] < SB_EXIT_DROP)

    def body(state):
        kb, _, accs, cs = state
        new_accs, cs = tiles_fn(kb, cs)
        accs = tuple(acc + new for acc, new in zip(accs, new_accs))
        return kb - 1, _min_all(cs), accs, cs

    state = lax.while_loop(cond, body, (first_tile, _min_all(cs), tuple(accs), tuple(cs)))
    return state[2]


def _sb_prompt_kernel(q_ref, k_ref, v_ref, u_ref, o_ref):
    nh, q_rows, hd = q_ref.shape
    tk = u_ref.shape[1]
    tq = tk
    nt = q_rows // tq
    first = pl.program_id(2) * nt
    u = u_ref[...]
    mask = _causal_mask(tq, tk)
    units = [(h, t) for t in range(nt) for h in range(nh)]
    qs = [q_ref[h, t * tq:(t + 1) * tq, :] for h, t in units]

    def keys(sel, start, n):
        return ([k_ref[h, pl.ds(start(t), n), :] for h, t in sel], [v_ref[h, pl.ds(start(t), n), :] for h, t in sel])

    def own_and_previous(sel, sel_qs):
        k2s, v2s = keys(sel, lambda t: pl.multiple_of((first + t - 1) * tk, tk), 2 * tk)
        return _sb_own_and_previous(sel_qs, k2s, v2s, u, mask)

    def sequence_start():
        ks, vs = keys(units[:nh], lambda t: 0, tk)
        accs, cs = _sb_tiles(qs[:nh], ks, vs, u, [0.0] * nh, mask)
        if nt > 1:
            more = own_and_previous(units[nh:], qs[nh:])
            accs, cs = accs + more[0], cs + more[1]
        return accs, cs

    accs, cs = lax.cond(first == 0, sequence_start, lambda: own_and_previous(units, qs))

    def sweep_rest():
        out = []
        for t in range(nt):
            sel = slice(t * nh, (t + 1) * nh)

            def tiles_fn(kb, cs_t, sel=sel):
                ks, vs = keys(units[sel], lambda _: pl.multiple_of(kb * tk, tk), tk)
                return _sb_tiles(qs[sel], ks, vs, u, cs_t, None)

            out.extend(_sb_sweep_rest(first + t - 2, tiles_fn, accs[sel], cs[sel]))
        return tuple(out)

    accs = lax.cond(_min_all(cs) < SB_EXIT_DROP, sweep_rest, lambda: accs)
    for (h, t), acc in zip(units, accs):
        o_ref[t * tq:(t + 1) * tq, h * hd:(h + 1) * hd] = acc.astype(o_ref.dtype)


def _sb_prompt(q, k, v, u):
    b, nh, t, hd = q.shape
    tq = u.shape[1]
    nt = SB_QUERY_TILES_PER_STEP if (t // tq) % SB_QUERY_TILES_PER_STEP == 0 else 1
    steps = t // (tq * nt)
    hp = SB_HEADS_PER_STEP
    kv_spec = pl.BlockSpec((None, hp, t, hd), lambda bi, hg, p: (bi, hg, 0, 0))
    return pl.pallas_call(
        _sb_prompt_kernel,
        grid=(b, nh // hp, steps),
        in_specs=[pl.BlockSpec((None, hp, nt * tq, hd), lambda bi, hg, p: (bi, hg, p, 0)), kv_spec, kv_spec,
                  _const_spec(u.shape)],
        out_specs=pl.BlockSpec((nt * tq, hp * hd), lambda bi, hg, p: (bi * steps + p, hg)),
        out_shape=jax.ShapeDtypeStruct((b * t, nh * hd), BF16),
        compiler_params=_params(("parallel", "parallel", "arbitrary")),
        name="sb_prompt",
    )(q, k, v, u)


def _sb_sample_kernel(q_ref, kn_ref, vn_ref, kp_ref, vp_ref, u_ref, us_ref, o_ref):
    nh, t, hd = q_ref.shape
    tk = u_ref.shape[1]
    n_past = kp_ref.shape[1] // tk
    u = u_ref[...]
    qs = [q_ref[h] for h in range(nh)]

    def past_tiles(kb, cs):
        start = kb * tk if isinstance(kb, int) else pl.multiple_of(kb * tk, tk)
        ks = [kp_ref[h, pl.ds(start, tk), :].astype(BF16) for h in range(nh)]
        vs = [vp_ref[h, pl.ds(start, tk), :].astype(BF16) for h in range(nh)]
        return _sb_tiles(qs, ks, vs, u, cs, None)

    acc_new, cs = _sb_tiles(qs, [kn_ref[h] for h in range(nh)], [vn_ref[h] for h in range(nh)], us_ref[...],
                            [0.0] * nh, _causal_mask(t, t))
    acc_past, cs = past_tiles(n_past - 1, cs)
    accs = _sb_sweep_rest(n_past - 2, past_tiles, [a + b for a, b in zip(acc_new, acc_past)], cs)
    for h in range(nh):
        o_ref[:, h * hd:(h + 1) * hd] = accs[h].astype(o_ref.dtype)


def _sb_sample(q, k_new, v_new, k_past, v_past, u, u_small):
    b, nh, t, hd = q.shape
    past = k_past.shape[2]
    new_spec = pl.BlockSpec((None, nh, t, hd), lambda bi: (bi, 0, 0, 0))
    past_spec = pl.BlockSpec((None, nh, past, hd), lambda bi: (bi, 0, 0, 0))
    return pl.pallas_call(
        _sb_sample_kernel,
        grid=(b,),
        in_specs=[new_spec, new_spec, new_spec, past_spec, past_spec, _const_spec(u.shape),
                  _const_spec(u_small.shape)],
        out_specs=pl.BlockSpec((t, nh * hd), lambda bi: (bi, 0)),
        out_shape=jax.ShapeDtypeStruct((b * t, nh * hd), BF16),
        compiler_params=_params(("parallel",)),
        name="sb_sample",
    )(q, k_new, v_new, k_past, v_past, u, u_small)


def _suffix_sum_matrix(n):
    r = lax.broadcasted_iota(jnp.int32, (n, n), 0)
    c = lax.broadcasted_iota(jnp.int32, (n, n), 1)
    return (r > c).astype(BF16)


def _in_kernel(x_ref, g_ref, wqkv_ref, w_ref, cw_ref, cb_ref, wg_ref, ba_ref, bx_ref, al_ref, cs_ref, h0_ref,
               xn_ref, q_ref, kb_ref, vb_ref, kf_ref, vf_ref, o_ref, nc_ref, hl_ref, xt, ht, tailbuf, hcar):
    batch, steps, d = x_ref.shape
    width = o_ref.shape[2]
    rows = batch * steps
    slabs, _, lanes = xt.shape
    pitch = xt.shape[1] // batch
    groups = batch // V7X_SUBLANES
    tail_rows = (CONV_WIDTH - 1) * batch
    i = pl.program_id(0)
    last = pl.num_programs(0) - 1

    @pl.when(i == 0)
    def _():
        tailbuf[...] = cs_ref[...]
        hcar[...] = h0_ref[...]

    xn = _rms(x_ref[...].reshape(rows, d), g_ref[...]).astype(BF16)
    xn_ref[...] = xn.reshape(batch, steps, d)
    xl = _dot(xn, w_ref[:, :width])
    for j in range(slabs):
        for b in range(batch):
            xt[j, b * pitch:b * pitch + steps, :] = xl[b * steps:(b + 1) * steps, j * lanes:(j + 1) * lanes]

    nh, hd = q_ref.shape[1], q_ref.shape[3]
    q_scale = LOG2E / math.sqrt(hd)
    half = width // 2
    gl_halves = []

    def project_gate_half(k):
        gl_halves.append(_dot(xn, w_ref[:, width + k * half:width + (k + 1) * half]))

    def project_heads(part, k, b_ref, f_ref):
        res = _dot(xn, wqkv_ref[:, part * nh * hd + k * half:part * nh * hd + (k + 1) * half])
        for h in range(half // hd):
            blk = res[:, h * hd:(h + 1) * hd].reshape(batch, steps, hd)
            if f_ref is None:
                blk = blk * q_scale
            else:
                f_ref[:, k * (half // hd) + h] = blk
            b_ref[:, k * (half // hd) + h] = blk.astype(BF16)

    filler = [functools.partial(project_gate_half, 0), functools.partial(project_gate_half, 1)]
    for part, (b_ref, f_ref) in enumerate(((q_ref, None), (kb_ref, kf_ref), (vb_ref, vf_ref))):
        filler += [functools.partial(project_heads, part, k, b_ref, f_ref) for k in range(2)]
    assert len(filler) == slabs

    al = al_ref[...]
    log_sig8 = LRU_C * (jnp.minimum(al, 0.0) - jnp.log(1.0 + jnp.exp(-jnp.abs(al))))
    k1 = -2.0 * 0.7978845608028654 * LOG2E

    assert wg_ref.shape[0] == slabs and wg_ref.shape[1] == lanes
    for j in range(slabs):
        cols = slice(j * lanes, (j + 1) * lanes)
        gathered = [jnp.concatenate([xt[j, pl.ds(g * V7X_SUBLANES * pitch + s, V7X_SUBLANES, stride=pitch), :]
                                     for g in range(groups)], axis=0) for s in range(steps)]
        xl_tm = jnp.concatenate([tailbuf[:, cols]] + gathered, axis=0)
        tailbuf[:, cols] = xl_tm[rows:rows + tail_rows, :]
        xc = cb_ref[:, cols]
        for tap in range(CONV_WIDTH):
            xc = xc + xl_tm[tap * batch:tap * batch + rows, :] * cw_ref[tap:tap + 1, cols]
        gates = _dot(xc.astype(BF16), wg_ref[j])
        filler[j]()

        r = _sigmoid(gates[:, :lanes] + ba_ref[:, cols])
        ig = _sigmoid(gates[:, lanes:] + bx_ref[:, cols])
        log_a = r * log_sig8[:, cols]
        a = jnp.exp(log_a)
        th = jnp.tanh(log_a)
        m2 = -2.0 * th / (1.0 - th)
        bterm = jnp.where(m2 > 0.0, m2 * lax.rsqrt(m2), 0.0) * (ig * xc)

        h = hcar[:, cols]
        for s in range(steps):
            h = a[s * batch:(s + 1) * batch] * h + bterm[s * batch:(s + 1) * batch]
            for g in range(groups):
                ht[j, pl.ds(g * V7X_SUBLANES * pitch + s, V7X_SUBLANES, stride=pitch), :] = (
                    h[g * V7X_SUBLANES:(g + 1) * V7X_SUBLANES])
        hcar[:, cols] = h

        hs = jnp.concatenate([ht[j, b * pitch:b * pitch + steps, :] for b in range(batch)], axis=0)
        off = (j * lanes) % half
        glj = gl_halves[j * lanes // half][:, off:off + lanes]
        gate = glj / (1.0 + jnp.exp2(glj * (k1 + (k1 * 0.044715) * (glj * glj))))
        o_ref[:, :, cols] = (hs * gate).astype(o_ref.dtype).reshape(batch, steps, lanes)

    @pl.when(i == last)
    def _():
        nc_ref[...] = tailbuf[...]
        hl_ref[...] = hcar[...]


def _in_proj(x, g, w_qkv, w_lru, conv_w, conv_b, w_gates, ba, bx, a_logit, conv_state_tm, h0):
    batch, seq, d = x.shape
    nh, hd = SB_HEADS, SB_HEAD_DIM
    width = w_lru.shape[1] // 2
    tail_rows = (CONV_WIDTH - 1) * batch
    assert batch % V7X_SUBLANES == 0
    steps = max(1, min(seq, TOKEN_TILE // batch))
    pitch = steps + V7X_SUBLANES if (steps // V7X_SUBLANES) % 2 == 0 else steps
    slab = (width // V7X_LANES, batch * pitch, V7X_LANES)
    tok_of = lambda w: pl.BlockSpec((batch, steps, w), lambda i: (0, i, 0))
    head_spec = pl.BlockSpec((batch, nh, steps, hd), lambda i: (0, 0, i, 0))
    head_bf = jax.ShapeDtypeStruct((batch, nh, seq, hd), BF16)
    head_f32 = jax.ShapeDtypeStruct((batch, nh, seq, hd), F32)
    return pl.pallas_call(
        _in_kernel,
        grid=(seq // steps,),
        in_specs=[tok_of(d), _const_spec((1, d)), _const_spec(w_qkv.shape), _const_spec(w_lru.shape),
                  _const_spec(conv_w.shape), _const_spec((1, width)), _const_spec(w_gates.shape),
                  _const_spec((1, width)), _const_spec((1, width)), _const_spec((1, width)),
                  _const_spec((tail_rows, width)), _const_spec((batch, width))],
        out_specs=[tok_of(d), head_spec, head_spec, head_spec, head_spec, head_spec, tok_of(width),
                   pl.BlockSpec((tail_rows, width), lambda i: (0, 0)),
                   pl.BlockSpec((batch, width), lambda i: (0, 0))],
        out_shape=[jax.ShapeDtypeStruct((batch, seq, d), BF16), head_bf, head_bf, head_bf, head_f32, head_f32,
                   jax.ShapeDtypeStruct((batch, seq, width), BF16),
                   jax.ShapeDtypeStruct((tail_rows, width), F32),
                   jax.ShapeDtypeStruct((batch, width), F32)],
        scratch_shapes=[pltpu.VMEM(slab, F32), pltpu.VMEM(slab, F32), pltpu.VMEM((tail_rows, width), F32),
                        pltpu.VMEM((batch, width), F32)],
        compiler_params=_params(("arbitrary",)),
        name="in_proj",
    )(x, g, w_qkv, w_lru, conv_w, conv_b, w_gates, ba, bx, a_logit, conv_state_tm, h0)


def _memkv_kernel(mem_ref, g_ref, wk_ref, wv_ref, kg_ref, k_ref, v_ref):
    nh, n, hd = k_ref.shape
    m = _rms(mem_ref[...], g_ref[...]).astype(BF16)
    k = _dot(m, wk_ref[...])
    v = _dot(m, wv_ref[...])
    for h in range(nh):
        k_ref[h] = _rms(k[:, h * hd:(h + 1) * hd], kg_ref[...])
        v_ref[h] = v[:, h * hd:(h + 1) * hd]


def _memory_kv(mem, g, wk, wv, kg):
    b, n, d = mem.shape
    nh = MEM_HEADS
    hd = wk.shape[1] // nh
    out = jax.ShapeDtypeStruct((b, nh, n, hd), F32)
    out_spec = pl.BlockSpec((None, nh, n, hd), lambda i: (i, 0, 0, 0))
    return pl.pallas_call(
        _memkv_kernel,
        grid=(b,),
        in_specs=[pl.BlockSpec((None, n, d), lambda i: (i, 0, 0)), _const_spec((1, d)), _const_spec(wk.shape),
                  _const_spec(wv.shape), _const_spec((1, hd))],
        out_specs=[out_spec, out_spec],
        out_shape=[out, out],
        compiler_params=_params(("parallel",)),
        name="memory_kv",
    )(mem, g, wk, wv, kg)


def _merge_kernel(x_ref, xn_ref, osb_ref, olru_ref, mk_ref, mv_ref, wqm_ref, qg_ref, wg_ref, bm_ref,
                  wsb_ref, wlru_ref, wmem_ref, wout_ref, o_ref):
    nb, nh, n_mem, hd = mk_ref.shape
    tm, d = o_ref.shape
    per_seq = tm // nb
    xn = xn_ref[...]

    qm = _dot(xn, wqm_ref[...])
    seqs = []
    for j in range(nb):
        heads = []
        for h in range(nh):
            qh = _rms(qm[j * per_seq:(j + 1) * per_seq, h * hd:(h + 1) * hd], qg_ref[...]).astype(BF16)
            s = _dot_nt(qh, mk_ref[j, h].astype(BF16)) / math.sqrt(hd)
            e = jnp.exp(s - jnp.max(s, axis=-1, keepdims=True))
            p = e / jnp.sum(e, axis=-1, keepdims=True)
            heads.append(_dot(p.astype(BF16), mv_ref[j, h].astype(BF16)))
        seqs.append(jnp.concatenate(heads, axis=1))
    omem = jnp.concatenate(seqs, axis=0).astype(BF16)

    gates = _sigmoid(_dot(xn, wg_ref[...]) + bm_ref[...])
    merged = (gates[:, :d] * _dot(osb_ref[...], wsb_ref[...])
              + gates[:, d:2 * d] * _dot(olru_ref[...], wlru_ref[...])
              + gates[:, 2 * d:] * _dot(omem, wmem_ref[...]))
    o_ref[...] = x_ref[...] + _dot(merged.astype(BF16), wout_ref[...])


def _merge(x2, xn, osb, olru, mk, mv, wqm, qg, wg, bm, wsb, wlru, wmem, wout, seq):
    m, d = x2.shape
    _, nh, n_mem, hd = mk.shape
    if seq >= TOKEN_TILE:
        tm, nb = TOKEN_TILE, 1
        per_batch = seq // tm
        mem_spec = pl.BlockSpec((nb, nh, n_mem, hd), lambda i: (i // per_batch, 0, 0, 0))
    else:
        nb = MERGE_SHORT_SEQUENCES
        tm = nb * seq
        mem_spec = pl.BlockSpec((nb, nh, n_mem, hd), lambda i: (i, 0, 0, 0))
    tok_of = lambda width: pl.BlockSpec((tm, width), lambda i: (i, 0))
    return pl.pallas_call(
        _merge_kernel,
        grid=(m // tm,),
        in_specs=[tok_of(d), tok_of(d), tok_of(osb.shape[1]), tok_of(olru.shape[1]), mem_spec, mem_spec,
                  _const_spec(wqm.shape), _const_spec((1, hd)), _const_spec(wg.shape),
                  _const_spec((1, N_BRANCH * d)), _const_spec(wsb.shape), _const_spec(wlru.shape),
                  _const_spec(wmem.shape), _const_spec(wout.shape)],
        out_specs=tok_of(d),
        out_shape=jax.ShapeDtypeStruct((m, d), F32),
        compiler_params=_params(("parallel",)),
        name="merge",
    )(x2, xn, osb, olru, mk, mv, wqm, qg, wg, bm, wsb, wlru, wmem, wout)


def _ffn_kernel(x_ref, g_ref, wgate_ref, wup_ref, wdown_ref, o_ref):
    x = x_ref[...]
    xn = _rms(x, g_ref[...]).astype(BF16)
    gate = _dot(xn, wgate_ref[...])
    up = _dot(xn, wup_ref[...])
    hidden = (gate * _sigmoid(gate) * up).astype(BF16)
    o_ref[...] = x + _dot(hidden, wdown_ref[...])


def _ffn(x1, g, wgate, wup, wdown):
    m, d = x1.shape
    tm = min(TOKEN_TILE, m)
    tok = pl.BlockSpec((tm, d), lambda i: (i, 0))
    return pl.pallas_call(
        _ffn_kernel,
        grid=(m // tm,),
        in_specs=[tok, _const_spec((1, d)), _const_spec(wgate.shape), _const_spec(wup.shape),
                  _const_spec(wdown.shape)],
        out_specs=tok,
        out_shape=jax.ShapeDtypeStruct((m, d), F32),
        compiler_params=_params(("parallel",)),
        name="ffn",
    )(x1, g, wgate, wup, wdown)


def _layer(x, past_k, past_v, conv_state, h0, mem_k, mem_v, lw):
    b, t, d = x.shape
    x2 = x.reshape(b * t, d)
    sbw = SB_HEADS * SB_HEAD_DIM
    lru_w = lw["conv_w"].shape[1]
    tail = CONV_WIDTH - 1
    row = lambda v: v.reshape(1, -1)

    lru0 = 3 * sbw
    w_gates = jnp.concatenate([lw["lru_wa"], lw["lru_wx"]], axis=2)
    xn, q, k_bf, v_bf, k_new, v_new, o_lru, new_conv_tm, h_last = _in_proj(
        x, row(lw["norm_mix_g"]), lw["w_in"][:, :lru0], lw["w_in"][:, lru0:lru0 + 2 * lru_w], lw["conv_w"],
        row(lw["conv_b"]), w_gates, row(lw["lru_ba"]), row(lw["lru_bx"]), row(lw["lru_a_logit"]),
        conv_state.transpose(1, 0, 2).reshape(tail * b, lru_w), h0)
    new_conv = new_conv_tm.reshape(tail, b, lru_w).transpose(1, 0, 2)

    if past_k is None:
        o_sb = _sb_prompt(q, k_bf, v_bf, _suffix_sum_matrix(min(SB_TILE, t)))
    else:
        o_sb = _sb_sample(q, k_bf, v_bf, past_k, past_v, _suffix_sum_matrix(SB_TILE), _suffix_sum_matrix(t))

    qm0 = lru0 + 2 * lru_w
    mem_w = lw["w_br_mem"].shape[0]
    x1 = _merge(x2, xn.reshape(b * t, d), o_sb, o_lru.reshape(b * t, lru_w), mem_k, mem_v,
                lw["w_in"][:, qm0:qm0 + mem_w], row(lw["q_norm_g"]),
                lw["w_in"][:, qm0 + mem_w:], row(lw["b_merge"]), lw["w_br_sb"], lw["w_br_lru"], lw["w_br_mem"],
                lw["w_out"], t)
    y = _ffn(x1, row(lw["norm_ffn_g"]), lw["w_ffn_gate"], lw["w_ffn_up"], lw["w_ffn_down"])
    return y.reshape(b, t, d), k_new, v_new, new_conv, h_last


_BF16_WEIGHTS = ("w_in", "lru_wa", "lru_wx", "w_br_sb", "w_br_lru", "w_br_mem", "w_out", "w_ffn_gate",
                 "w_ffn_up", "w_ffn_down")


def kernel(x_prompt, x_sample, mem_prompt, cache_sb_k, cache_sb_v, state_conv, state_lru_h, cache_mem_k, cache_mem_v, norm_mix_g, w_in, b_merge, conv_w, conv_b, lru_wa, lru_ba, lru_wx, lru_bx, lru_a_logit, q_norm_g, k_norm_g, mem_norm_g, w_mem_k, w_mem_v, w_br_sb, w_br_lru, w_br_mem, w_out, norm_ffn_g, w_ffn_gate, w_ffn_up, w_ffn_down):
    depth = w_in.shape[0]
    bp = x_prompt.shape[0]
    lru_w = conv_w.shape[2]
    hp, hs = x_prompt, x_sample
    outs = [[] for _ in range(10)]
    for l in range(depth):
        lw = {
            "norm_mix_g": norm_mix_g[l], "w_in": w_in[l], "b_merge": b_merge[l], "conv_w": conv_w[l],
            "conv_b": conv_b[l], "lru_wa": lru_wa[l], "lru_ba": lru_ba[l], "lru_wx": lru_wx[l],
            "lru_bx": lru_bx[l], "lru_a_logit": lru_a_logit[l], "q_norm_g": q_norm_g[l],
            "w_br_sb": w_br_sb[l], "w_br_lru": w_br_lru[l], "w_br_mem": w_br_mem[l], "w_out": w_out[l],
            "norm_ffn_g": norm_ffn_g[l], "w_ffn_gate": w_ffn_gate[l], "w_ffn_up": w_ffn_up[l],
            "w_ffn_down": w_ffn_down[l],
        }
        for name in _BF16_WEIGHTS:
            lw[name] = lw[name].astype(BF16)
        mk_p, mv_p = _memory_kv(mem_prompt, mem_norm_g[l].reshape(1, -1), w_mem_k[l].astype(BF16),
                                w_mem_v[l].astype(BF16), k_norm_g[l].reshape(1, -1))
        hp, k_p, v_p, c_p, s_p = _layer(hp, None, None, jnp.zeros((bp, CONV_WIDTH - 1, lru_w), F32),
                                        jnp.zeros((bp, lru_w), F32), mk_p, mv_p, lw)
        hs, k_s, v_s, c_s, s_s = _layer(hs, cache_sb_k[l], cache_sb_v[l], state_conv[l], state_lru_h[l],
                                        cache_mem_k[l], cache_mem_v[l], lw)
        for lst, val in zip(outs, (k_p, v_p, c_p, s_p, mk_p, mv_p, k_s, v_s, c_s, s_s)):
            lst.append(val)
    return (hp, hs) + tuple(lst[0][None] if depth == 1 else jnp.stack(lst) for lst in outs)
```

```python
import functools
import math

import jax
import jax.numpy as jnp
from jax import lax
from jax.experimental import pallas as pl
from jax.experimental.pallas import tpu as pltpu

F32 = jnp.float32
BF16 = jnp.bfloat16

EPS = 1e-6
LOG2E = 1.4426950408889634
SB_HEADS = 8
SB_HEAD_DIM = 128
LRU_BLOCKS = 8
LRU_C = 8.0
CONV_WIDTH = 4
MEM_HEADS = 4
N_BRANCH = 3

V7X_SUBLANES = 8
V7X_LANES = 128
V7X_MXU_DIM = 256
V7X_VMEM_LIMIT_BYTES = 62 * 1024 * 1024

SB_TILE = V7X_MXU_DIM
SB_HEADS_PER_STEP = 4
SB_QUERY_TILES_PER_STEP = 2
SB_EXIT_DROP = 150.0
TOKEN_TILE = 512
MERGE_SHORT_SEQUENCES = 4


def _dot(a, b):
    return jnp.dot(a, b, preferred_element_type=F32)


def _dot_nt(a, b):
    return lax.dot_general(a, b, (((1,), (1,)), ((), ())), preferred_element_type=F32)


def _sigmoid(x):
    return 1.0 / (1.0 + jnp.exp(-x))


def _rms(x, g):
    ms = jnp.mean(x * x, axis=-1, keepdims=True)
    return x * lax.rsqrt(ms + EPS) * g


def _const_spec(shape):
    zeros = (0,) * len(shape)
    return pl.BlockSpec(shape, lambda *_: zeros, pipeline_mode=pl.Buffered(1))


def _params(semantics):
    return pltpu.CompilerParams(dimension_semantics=semantics, vmem_limit_bytes=V7X_VMEM_LIMIT_BYTES)


def _sb_scores(q, k):
    z = _dot_nt(q, k)
    neg, pos = jnp.minimum(z, 0.0), jnp.maximum(z, 0.0)
    tail = jnp.log2(1.0 + jnp.exp2(neg - pos))
    return neg - tail, pos + tail


def _suffix_sums(drop, u):
    return _dot(drop.astype(BF16), u)


def _row_sum(x):
    return jnp.sum(x, axis=-1, keepdims=True)


def _sb_tiles(qs, ks, vs, u, cs, mask):
    scores = [_sb_scores(q, k) for q, k in zip(qs, ks)]
    drops = [drop if mask is None else jnp.where(mask, drop, 0.0) for _, drop in scores]
    sums = [_suffix_sums(drop, u) for drop in drops]
    accs, new_cs = [], []
    for (log_beta, _), drop, s, v, c in zip(scores, drops, sums, vs, cs):
        w = jnp.exp2(log_beta - s - c)
        if mask is not None:
            w = jnp.where(mask, w, 0.0)
        accs.append(_dot(w.astype(BF16), v))
        new_cs.append(c + _row_sum(drop))
    return tuple(accs), tuple(new_cs)


def _sb_own_and_previous(qs, k2s, v2s, u, mask, prev_valid=None, between_stages=(None, None)):
    tk = mask.shape[1]
    prev_valid = prev_valid or [None] * len(qs)
    scores = [_sb_scores(q, k2) for q, k2 in zip(qs, k2s)]
    drops = [(jnp.where(mask, drop[:, tk:], 0.0),
              drop[:, :tk] if ok is None else jnp.where(ok, drop[:, :tk], 0.0))
             for (_, drop), ok in zip(scores, prev_valid)]
    if between_stages[0] is not None:
        between_stages[0]()
    sums = [(_suffix_sums(own, u), _suffix_sums(prev, u)) for own, prev in drops]
    if between_stages[1] is not None:
        between_stages[1]()
    accs, cs = [], []
    for (log_beta, _), (drop_own, drop_prev), (sum_own, sum_prev), v2, ok in zip(scores, drops, sums, v2s,
                                                                                 prev_valid):
        c_own = _row_sum(drop_own)
        w_own = jnp.where(mask, jnp.exp2(log_beta[:, tk:] - sum_own), 0.0)
        w_prev = jnp.exp2(log_beta[:, :tk] - sum_prev - c_own)
        if ok is not None:
            w_prev = jnp.where(ok, w_prev, 0.0)
        w = jnp.concatenate([w_prev, w_own], axis=1).astype(BF16)
        accs.append(_dot(w, v2))
        cs.append(c_own + _row_sum(drop_prev))
    return tuple(accs), tuple(cs)


def _causal_mask(tq, tk):
    row = lax.broadcasted_iota(jnp.int32, (tq, tk), 0)
    col = lax.broadcasted_iota(jnp.int32, (tq, tk), 1)
    return col < row


def _min_all(cs):
    return functools.reduce(jnp.minimum, [jnp.min(c) for c in cs])


def _sb_sweep_rest(first_tile, tiles_fn, accs, cs):
    def cond(state):
        return jnp.logical_and(state[0] >= 0, state[1] < SB_EXIT_DROP)

    def body(state):
        kb, _, accs, cs = state
        new_accs, cs = tiles_fn(kb, cs)
        accs = tuple(acc + new for acc, new in zip(accs, new_accs))
        return kb - 1, _min_all(cs), accs, cs

    state = lax.while_loop(cond, body, (first_tile, _min_all(cs), tuple(accs), tuple(cs)))
    return state[2]


def _sb_prompt_kernel(q_ref, k_ref, v_ref, u_ref, o_ref):
    nh, q_rows, hd = q_ref.shape
    tk = u_ref.shape[1]
    tq = tk
    nt = q_rows // tq
    first = pl.program_id(2) * nt
    u = u_ref[...]
    mask = _causal_mask(tq, tk)
    units = [(h, t) for t in range(nt) for h in range(nh)]
    qs = [q_ref[h, t * tq:(t + 1) * tq, :] for h, t in units]

    def keys(sel, start, n):
        return ([k_ref[h, pl.ds(start(t), n), :] for h, t in sel], [v_ref[h, pl.ds(start(t), n), :] for h, t in sel])

    def own_and_previous(sel, sel_qs):
        k2s, v2s = keys(sel, lambda t: pl.multiple_of((first + t - 1) * tk, tk), 2 * tk)
        return _sb_own_and_previous(sel_qs, k2s, v2s, u, mask)

    def sequence_start():
        ks, vs = keys(units[:nh], lambda t: 0, tk)
        accs, cs = _sb_tiles(qs[:nh], ks, vs, u, [0.0] * nh, mask)
        if nt > 1:
            more = own_and_previous(units[nh:], qs[nh:])
            accs, cs = accs + more[0], cs + more[1]
        return accs, cs

    accs, cs = lax.cond(first == 0, sequence_start, lambda: own_and_previous(units, qs))

    def sweep_rest():
        out = []
        for t in range(nt):
            sel = slice(t * nh, (t + 1) * nh)

            def tiles_fn(kb, cs_t, sel=sel):
                ks, vs = keys(units[sel], lambda _: pl.multiple_of(kb * tk, tk), tk)
                return _sb_tiles(qs[sel], ks, vs, u, cs_t, None)

            out.extend(_sb_sweep_rest(first + t - 2, tiles_fn, accs[sel], cs[sel]))
        return tuple(out)

    accs = lax.cond(_min_all(cs) < SB_EXIT_DROP, sweep_rest, lambda: accs)
    for (h, t), acc in zip(units, accs):
        o_ref[t * tq:(t + 1) * tq, h * hd:(h + 1) * hd] = acc.astype(o_ref.dtype)


def _sb_prompt(q, k, v, u):
    b, nh, t, hd = q.shape
    tq = u.shape[1]
    nt = SB_QUERY_TILES_PER_STEP if (t // tq) % SB_QUERY_TILES_PER_STEP == 0 else 1
    steps = t // (tq * nt)
    hp = SB_HEADS_PER_STEP
    kv_spec = pl.BlockSpec((None, hp, t, hd), lambda bi, hg, p: (bi, hg, 0, 0))
    return pl.pallas_call(
        _sb_prompt_kernel,
        grid=(b, nh // hp, steps),
        in_specs=[pl.BlockSpec((None, hp, nt * tq, hd), lambda bi, hg, p: (bi, hg, p, 0)), kv_spec, kv_spec,
                  _const_spec(u.shape)],
        out_specs=pl.BlockSpec((nt * tq, hp * hd), lambda bi, hg, p: (bi * steps + p, hg)),
        out_shape=jax.ShapeDtypeStruct((b * t, nh * hd), BF16),
        compiler_params=_params(("parallel", "parallel", "arbitrary")),
        name="sb_prompt",
    )(q, k, v, u)


def _sb_sample_kernel(q_ref, kn_ref, vn_ref, kp_ref, vp_ref, u_ref, us_ref, o_ref):
    nh, t, hd = q_ref.shape
    tk = u_ref.shape[1]
    n_past = kp_ref.shape[1] // tk
    u = u_ref[...]
    qs = [q_ref[h] for h in range(nh)]

    def past_tiles(kb, cs):
        start = kb * tk if isinstance(kb, int) else pl.multiple_of(kb * tk, tk)
        ks = [kp_ref[h, pl.ds(start, tk), :].astype(BF16) for h in range(nh)]
        vs = [vp_ref[h, pl.ds(start, tk), :].astype(BF16) for h in range(nh)]
        return _sb_tiles(qs, ks, vs, u, cs, None)

    acc_new, cs = _sb_tiles(qs, [kn_ref[h] for h in range(nh)], [vn_ref[h] for h in range(nh)], us_ref[...],
                            [0.0] * nh, _causal_mask(t, t))
    acc_past, cs = past_tiles(n_past - 1, cs)
    accs = _sb_sweep_rest(n_past - 2, past_tiles, [a + b for a, b in zip(acc_new, acc_past)], cs)
    for h in range(nh):
        o_ref[:, h * hd:(h + 1) * hd] = accs[h].astype(o_ref.dtype)


def _sb_sample(q, k_new, v_new, k_past, v_past, u, u_small):
    b, nh, t, hd = q.shape
    past = k_past.shape[2]
    new_spec = pl.BlockSpec((None, nh, t, hd), lambda bi: (bi, 0, 0, 0))
    past_spec = pl.BlockSpec((None, nh, past, hd), lambda bi: (bi, 0, 0, 0))
    return pl.pallas_call(
        _sb_sample_kernel,
        grid=(b,),
        in_specs=[new_spec, new_spec, new_spec, past_spec, past_spec, _const_spec(u.shape),
                  _const_spec(u_small.shape)],
        out_specs=pl.BlockSpec((t, nh * hd), lambda bi: (bi, 0)),
        out_shape=jax.ShapeDtypeStruct((b * t, nh * hd), BF16),
        compiler_params=_params(("parallel",)),
        name="sb_sample",
    )(q, k_new, v_new, k_past, v_past, u, u_small)


def _suffix_sum_matrix(n):
    r = lax.broadcasted_iota(jnp.int32, (n, n), 0)
    c = lax.broadcasted_iota(jnp.int32, (n, n), 1)
    return (r > c).astype(BF16)


def _in_kernel(x_ref, g_ref, wqkv_ref, w_ref, cw_ref, cb_ref, wg_ref, ba_ref, bx_ref, al_ref, cs_ref, h0_ref,
               xn_ref, q_ref, kb_ref, vb_ref, kf_ref, vf_ref, o_ref, nc_ref, hl_ref, xt, ht, tailbuf, hcar):
    batch, steps, d = x_ref.shape
    width = o_ref.shape[2]
    rows = batch * steps
    slabs, _, lanes = xt.shape
    pitch = xt.shape[1] // batch
    groups = batch // V7X_SUBLANES
    tail_rows = (CONV_WIDTH - 1) * batch
    i = pl.program_id(0)
    last = pl.num_programs(0) - 1

    @pl.when(i == 0)
    def _():
        tailbuf[...] = cs_ref[...]
        hcar[...] = h0_ref[...]

    xn = _rms(x_ref[...].reshape(rows, d), g_ref[...]).astype(BF16)
    xn_ref[...] = xn.reshape(batch, steps, d)
    xl = _dot(xn, w_ref[:, :width])
    for j in range(slabs):
        for b in range(batch):
            xt[j, b * pitch:b * pitch + steps, :] = xl[b * steps:(b + 1) * steps, j * lanes:(j + 1) * lanes]

    nh, hd = q_ref.shape[1], q_ref.shape[3]
    q_scale = LOG2E / math.sqrt(hd)
    half = width // 2
    gl_halves = []

    def project_gate_half(k):
        gl_halves.append(_dot(xn, w_ref[:, width + k * half:width + (k + 1) * half]))

    def project_heads(part, k, b_ref, f_ref):
        res = _dot(xn, wqkv_ref[:, part * nh * hd + k * half:part * nh * hd + (k + 1) * half])
        for h in range(half // hd):
            blk = res[:, h * hd:(h + 1) * hd].reshape(batch, steps, hd)
            if f_ref is None:
                blk = blk * q_scale
            else:
                f_ref[:, k * (half // hd) + h] = blk
            b_ref[:, k * (half // hd) + h] = blk.astype(BF16)

    filler = [functools.partial(project_gate_half, 0), functools.partial(project_gate_half, 1)]
    for part, (b_ref, f_ref) in enumerate(((q_ref, None), (kb_ref, kf_ref), (vb_ref, vf_ref))):
        filler += [functools.partial(project_heads, part, k, b_ref, f_ref) for k in range(2)]
    assert len(filler) == slabs

    al = al_ref[...]
    log_sig8 = LRU_C * (jnp.minimum(al, 0.0) - jnp.log(1.0 + jnp.exp(-jnp.abs(al))))
    k1 = -2.0 * 0.7978845608028654 * LOG2E

    assert wg_ref.shape[0] == slabs and wg_ref.shape[1] == lanes
    for j in range(slabs):
        cols = slice(j * lanes, (j + 1) * lanes)
        gathered = [jnp.concatenate([xt[j, pl.ds(g * V7X_SUBLANES * pitch + s, V7X_SUBLANES, stride=pitch), :]
                                     for g in range(groups)], axis=0) for s in range(steps)]
        xl_tm = jnp.concatenate([tailbuf[:, cols]] + gathered, axis=0)
        tailbuf[:, cols] = xl_tm[rows:rows + tail_rows, :]
        xc = cb_ref[:, cols]
        for tap in range(CONV_WIDTH):
            xc = xc + xl_tm[tap * batch:tap * batch + rows, :] * cw_ref[tap:tap + 1, cols]
        gates = _dot(xc.astype(BF16), wg_ref[j])
        filler[j]()

        r = _sigmoid(gates[:, :lanes] + ba_ref[:, cols])
        ig = _sigmoid(gates[:, lanes:] + bx_ref[:, cols])
        log_a = r * log_sig8[:, cols]
        a = jnp.exp(log_a)
        th = jnp.tanh(log_a)
        m2 = -2.0 * th / (1.0 - th)
        bterm = jnp.where(m2 > 0.0, m2 * lax.rsqrt(m2), 0.0) * (ig * xc)

        h = hcar[:, cols]
        for s in range(steps):
            h = a[s * batch:(s + 1) * batch] * h + bterm[s * batch:(s + 1) * batch]
            for g in range(groups):
                ht[j, pl.ds(g * V7X_SUBLANES * pitch + s, V7X_SUBLANES, stride=pitch), :] = (
                    h[g * V7X_SUBLANES:(g + 1) * V7X_SUBLANES])
        hcar[:, cols] = h

        hs = jnp.concatenate([ht[j, b * pitch:b * pitch + steps, :] for b in range(batch)], axis=0)
        off = (j * lanes) % half
        glj = gl_halves[j * lanes // half][:, off:off + lanes]
        gate = glj / (1.0 + jnp.exp2(glj * (k1 + (k1 * 0.044715) * (glj * glj))))
        o_ref[:, :, cols] = (hs * gate).astype(o_ref.dtype).reshape(batch, steps, lanes)

    @pl.when(i == last)
    def _():
        nc_ref[...] = tailbuf[...]
        hl_ref[...] = hcar[...]


def _in_proj(x, g, w_qkv, w_lru, conv_w, conv_b, w_gates, ba, bx, a_logit, conv_state_tm, h0):
    batch, seq, d = x.shape
    nh, hd = SB_HEADS, SB_HEAD_DIM
    width = w_lru.shape[1] // 2
    tail_rows = (CONV_WIDTH - 1) * batch
    assert batch % V7X_SUBLANES == 0
    steps = max(1, min(seq, TOKEN_TILE // batch))
    pitch = steps + V7X_SUBLANES if (steps // V7X_SUBLANES) % 2 == 0 else steps
    slab = (width // V7X_LANES, batch * pitch, V7X_LANES)
    tok_of = lambda w: pl.BlockSpec((batch, steps, w), lambda i: (0, i, 0))
    head_spec = pl.BlockSpec((batch, nh, steps, hd), lambda i: (0, 0, i, 0))
    head_bf = jax.ShapeDtypeStruct((batch, nh, seq, hd), BF16)
    head_f32 = jax.ShapeDtypeStruct((batch, nh, seq, hd), F32)
    return pl.pallas_call(
        _in_kernel,
        grid=(seq // steps,),
        in_specs=[tok_of(d), _const_spec((1, d)), _const_spec(w_qkv.shape), _const_spec(w_lru.shape),
                  _const_spec(conv_w.shape), _const_spec((1, width)), _const_spec(w_gates.shape),
                  _const_spec((1, width)), _const_spec((1, width)), _const_spec((1, width)),
                  _const_spec((tail_rows, width)), _const_spec((batch, width))],
        out_specs=[tok_of(d), head_spec, head_spec, head_spec, head_spec, head_spec, tok_of(width),
                   pl.BlockSpec((tail_rows, width), lambda i: (0, 0)),
                   pl.BlockSpec((batch, width), lambda i: (0, 0))],
        out_shape=[jax.ShapeDtypeStruct((batch, seq, d), BF16), head_bf, head_bf, head_bf, head_f32, head_f32,
                   jax.ShapeDtypeStruct((batch, seq, width), BF16),
                   jax.ShapeDtypeStruct((tail_rows, width), F32),
                   jax.ShapeDtypeStruct((batch, width), F32)],
        scratch_shapes=[pltpu.VMEM(slab, F32), pltpu.VMEM(slab, F32), pltpu.VMEM((tail_rows, width), F32),
                        pltpu.VMEM((batch, width), F32)],
        compiler_params=_params(("arbitrary",)),
        name="in_proj",
    )(x, g, w_qkv, w_lru, conv_w, conv_b, w_gates, ba, bx, a_logit, conv_state_tm, h0)


def _memkv_kernel(mem_ref, g_ref, wk_ref, wv_ref, kg_ref, k_ref, v_ref):
    nh, n, hd = k_ref.shape
    m = _rms(mem_ref[...], g_ref[...]).astype(BF16)
    k = _dot(m, wk_ref[...])
    v = _dot(m, wv_ref[...])
    for h in range(nh):
        k_ref[h] = _rms(k[:, h * hd:(h + 1) * hd], kg_ref[...])
        v_ref[h] = v[:, h * hd:(h + 1) * hd]


def _memory_kv(mem, g, wk, wv, kg):
    b, n, d = mem.shape
    nh = MEM_HEADS
    hd = wk.shape[1] // nh
    out = jax.ShapeDtypeStruct((b, nh, n, hd), F32)
    out_spec = pl.BlockSpec((None, nh, n, hd), lambda i: (i, 0, 0, 0))
    return pl.pallas_call(
        _memkv_kernel,
        grid=(b,),
        in_specs=[pl.BlockSpec((None, n, d), lambda i: (i, 0, 0)), _const_spec((1, d)), _const_spec(wk.shape),
                  _const_spec(wv.shape), _const_spec((1, hd))],
        out_specs=[out_spec, out_spec],
        out_shape=[out, out],
        compiler_params=_params(("parallel",)),
        name="memory_kv",
    )(mem, g, wk, wv, kg)


def _memory_attention(qm, mk_ref, mv_ref, j, qg):
    _, nh, _, hd = mk_ref.shape
    heads = []
    for h in range(nh):
        qh = _rms(qm[:, h * hd:(h + 1) * hd], qg).astype(BF16)
        s = _dot_nt(qh, mk_ref[j, h].astype(BF16)) / math.sqrt(hd)
        e = jnp.exp(s - jnp.max(s, axis=-1, keepdims=True))
        p = e / jnp.sum(e, axis=-1, keepdims=True)
        heads.append(_dot(p.astype(BF16), mv_ref[j, h].astype(BF16)))
    return jnp.concatenate(heads, axis=1)


def _merge_kernel(x_ref, xn_ref, osb_ref, olru_ref, mk_ref, mv_ref, wqm_ref, qg_ref, wg_ref, bm_ref,
                  wsb_ref, wlru_ref, wmem_ref, wout_ref, o_ref):
    nb = mk_ref.shape[0]
    tm, d = o_ref.shape
    per_seq = tm // nb
    xn = xn_ref[...]

    qm = _dot(xn, wqm_ref[...])
    omem = jnp.concatenate([_memory_attention(qm[j * per_seq:(j + 1) * per_seq], mk_ref, mv_ref, j, qg_ref[...])
                            for j in range(nb)], axis=0).astype(BF16)

    gates = _sigmoid(_dot(xn, wg_ref[...]) + bm_ref[...])
    merged = (gates[:, :d] * _dot(osb_ref[...], wsb_ref[...])
              + gates[:, d:2 * d] * _dot(olru_ref[...], wlru_ref[...])
              + gates[:, 2 * d:] * _dot(omem, wmem_ref[...]))
    o_ref[...] = x_ref[...] + _dot(merged.astype(BF16), wout_ref[...])


def _attn_merge_kernel(x_ref, xn_ref, olru_ref, q_ref, *refs, tiles_per_seq):
    nh, q_rows, hd = q_ref.shape
    nt = (len(refs) - 19) // 2
    k_tiles, v_tiles = refs[:nt + 1], refs[nt + 1:2 * nt + 2]
    (k_hbm, v_hbm, u_ref, mk_ref, mv_ref, wqm_ref, qg_ref, wg_ref, bm_ref, wsb_ref, wlru_ref, wmem_ref, wout_ref,
     o_ref, kbuf, vbuf, sem) = refs[2 * nt + 2:]
    tk = u_ref.shape[1]
    tq = tk
    assert nt == q_rows // tq
    d = o_ref.shape[1]
    seq_index = pl.program_id(0) // tiles_per_seq
    first = (pl.program_id(0) % tiles_per_seq) * nt
    u = u_ref[...]
    mask = _causal_mask(tq, tk)
    units = [(h, t) for t in range(nt) for h in range(nh)]
    qs = [q_ref[h, t * tq:(t + 1) * tq, :] for h, t in units]
    k2s = [jnp.concatenate([k_tiles[t][h], k_tiles[t + 1][h]], axis=0) for h, t in units]
    v2s = [jnp.concatenate([v_tiles[t][h], v_tiles[t + 1][h]], axis=0) for h, t in units]
    has_previous = first > 0
    prev_valid = [has_previous if t == 0 else None for _, t in units]

    xn = xn_ref[...]
    bm = bm_ref[...]
    early = {}

    def gate(n):
        return _sigmoid(_dot(xn, wg_ref[:, n * d:(n + 1) * d]) + bm[:, n * d:(n + 1) * d])

    def project_memory_queries():
        early["qm"] = _dot(xn, wqm_ref[...])

    def attend_memory():
        early["omem"] = _memory_attention(early.pop("qm"), mk_ref, mv_ref, 0, qg_ref[...]).astype(BF16)

    def recurrent_term():
        early["rest"] = gate(1) * _dot(olru_ref[...], wlru_ref[...])

    def memory_term():
        early["rest"] = early["rest"] + gate(2) * _dot(early.pop("omem"), wmem_ref[...])

    def sweep_gate():
        early["g0"] = gate(0)

    gaps = [project_memory_queries, attend_memory, recurrent_term, memory_term, sweep_gate]
    gaps += [None] * (2 * len(units) // SB_HEADS_PER_STEP - len(gaps))
    accs, cs = (), ()
    for start in range(0, len(units), SB_HEADS_PER_STEP):
        sel = slice(start, start + SB_HEADS_PER_STEP)
        acc_g, c_g = _sb_own_and_previous(qs[sel], k2s[sel], v2s[sel], u, mask, prev_valid[sel],
                                          (gaps.pop(0), gaps.pop(0)))
        accs, cs = accs + acc_g, cs + c_g
    assert not gaps

    def fetch(kb):
        start = pl.multiple_of(kb * tk, tk)
        copies = [pltpu.make_async_copy(k_hbm.at[seq_index, :, pl.ds(start, tk), :], kbuf, sem.at[0]),
                  pltpu.make_async_copy(v_hbm.at[seq_index, :, pl.ds(start, tk), :], vbuf, sem.at[1])]
        for copy in copies:
            copy.start()
        for copy in copies:
            copy.wait()

    def sweep_rest():
        out = []
        for t in range(nt):
            sel = slice(t * nh, (t + 1) * nh)

            def tiles_fn(kb, cs_t, sel=sel):
                fetch(kb)
                return _sb_tiles(qs[sel], [kbuf[h] for h in range(nh)], [vbuf[h] for h in range(nh)], u, cs_t,
                                 None)

            out.extend(_sb_sweep_rest(first + t - 2, tiles_fn, accs[sel], cs[sel]))
        return tuple(out)

    accs = lax.cond(_min_all(cs) < SB_EXIT_DROP, sweep_rest, lambda: accs)
    osb = jnp.concatenate([jnp.concatenate(accs[t * nh:(t + 1) * nh], axis=1) for t in range(nt)],
                          axis=0).astype(BF16)

    merged = early["g0"] * _dot(osb, wsb_ref[...]) + early["rest"]
    o_ref[...] = x_ref[...] + _dot(merged.astype(BF16), wout_ref[...])


def _attn_merge(x2, xn, olru, q, k, v, u, mk, mv, wqm, qg, wg, bm, wsb, wlru, wmem, wout):
    m, d = x2.shape
    batch, nh, seq, hd = q.shape
    _, nmh, n_mem, mhd = mk.shape
    tk = u.shape[1]
    nt = SB_QUERY_TILES_PER_STEP
    tm = nt * tk
    tiles_per_seq = seq // tm
    assert seq % tm == 0
    tok_of = lambda width: pl.BlockSpec((tm, width), lambda i: (i, 0))
    q_spec = pl.BlockSpec((None, nh, tm, hd), lambda i: (i // tiles_per_seq, 0, i % tiles_per_seq, 0))

    def key_tile(offset):
        return pl.BlockSpec((None, nh, tk, hd), lambda i: (i // tiles_per_seq, 0,
                                                           jnp.maximum((i % tiles_per_seq) * nt + offset, 0), 0))

    any_spec = pl.BlockSpec(memory_space=pl.ANY)
    mem_spec = pl.BlockSpec((1, nmh, n_mem, mhd), lambda i: (i // tiles_per_seq, 0, 0, 0))
    key_tiles = [key_tile(offset) for offset in range(-1, nt)]
    return pl.pallas_call(
        functools.partial(_attn_merge_kernel, tiles_per_seq=tiles_per_seq),
        grid=(m // tm,),
        in_specs=[tok_of(d), tok_of(d), tok_of(olru.shape[1]), q_spec, *key_tiles, *key_tiles, any_spec,
                  any_spec, _const_spec(u.shape), mem_spec,
                  mem_spec, _const_spec(wqm.shape), _const_spec((1, mhd)), _const_spec(wg.shape),
                  _const_spec((1, N_BRANCH * d)), _const_spec(wsb.shape), _const_spec(wlru.shape),
                  _const_spec(wmem.shape), _const_spec(wout.shape)],
        out_specs=tok_of(d),
        out_shape=jax.ShapeDtypeStruct((m, d), F32),
        scratch_shapes=[pltpu.VMEM((nh, tk, hd), BF16), pltpu.VMEM((nh, tk, hd), BF16),
                        pltpu.SemaphoreType.DMA((2,))],
        compiler_params=_params(("parallel",)),
        name="attn_merge",
    )(x2, xn, olru, q, *([k] * (nt + 1)), *([v] * (nt + 1)), k, v, u, mk, mv, wqm, qg, wg, bm, wsb, wlru, wmem,
      wout)


def _merge(x2, xn, osb, olru, mk, mv, wqm, qg, wg, bm, wsb, wlru, wmem, wout, seq):
    m, d = x2.shape
    _, nh, n_mem, hd = mk.shape
    if seq >= TOKEN_TILE:
        tm, nb = TOKEN_TILE, 1
        per_batch = seq // tm
        mem_spec = pl.BlockSpec((nb, nh, n_mem, hd), lambda i: (i // per_batch, 0, 0, 0))
    else:
        nb = MERGE_SHORT_SEQUENCES
        tm = nb * seq
        mem_spec = pl.BlockSpec((nb, nh, n_mem, hd), lambda i: (i, 0, 0, 0))
    tok_of = lambda width: pl.BlockSpec((tm, width), lambda i: (i, 0))
    return pl.pallas_call(
        _merge_kernel,
        grid=(m // tm,),
        in_specs=[tok_of(d), tok_of(d), tok_of(osb.shape[1]), tok_of(olru.shape[1]), mem_spec, mem_spec,
                  _const_spec(wqm.shape), _const_spec((1, hd)), _const_spec(wg.shape),
                  _const_spec((1, N_BRANCH * d)), _const_spec(wsb.shape), _const_spec(wlru.shape),
                  _const_spec(wmem.shape), _const_spec(wout.shape)],
        out_specs=tok_of(d),
        out_shape=jax.ShapeDtypeStruct((m, d), F32),
        compiler_params=_params(("parallel",)),
        name="merge",
    )(x2, xn, osb, olru, mk, mv, wqm, qg, wg, bm, wsb, wlru, wmem, wout)


def _ffn_kernel(x_ref, g_ref, wgate_ref, wup_ref, wdown_ref, o_ref):
    x = x_ref[...]
    xn = _rms(x, g_ref[...]).astype(BF16)
    gate = _dot(xn, wgate_ref[...])
    up = _dot(xn, wup_ref[...])
    hidden = (gate * _sigmoid(gate) * up).astype(BF16)
    o_ref[...] = x + _dot(hidden, wdown_ref[...])


def _ffn(x1, g, wgate, wup, wdown):
    m, d = x1.shape
    tm = min(TOKEN_TILE, m)
    tok = pl.BlockSpec((tm, d), lambda i: (i, 0))
    return pl.pallas_call(
        _ffn_kernel,
        grid=(m // tm,),
        in_specs=[tok, _const_spec((1, d)), _const_spec(wgate.shape), _const_spec(wup.shape),
                  _const_spec(wdown.shape)],
        out_specs=tok,
        out_shape=jax.ShapeDtypeStruct((m, d), F32),
        compiler_params=_params(("parallel",)),
        name="ffn",
    )(x1, g, wgate, wup, wdown)


def _layer(x, past_k, past_v, conv_state, h0, mem_k, mem_v, lw):
    b, t, d = x.shape
    x2 = x.reshape(b * t, d)
    sbw = SB_HEADS * SB_HEAD_DIM
    lru_w = lw["conv_w"].shape[1]
    tail = CONV_WIDTH - 1
    row = lambda v: v.reshape(1, -1)

    lru0 = 3 * sbw
    w_gates = jnp.concatenate([lw["lru_wa"], lw["lru_wx"]], axis=2)
    xn, q, k_bf, v_bf, k_new, v_new, o_lru, new_conv_tm, h_last = _in_proj(
        x, row(lw["norm_mix_g"]), lw["w_in"][:, :lru0], lw["w_in"][:, lru0:lru0 + 2 * lru_w], lw["conv_w"],
        row(lw["conv_b"]), w_gates, row(lw["lru_ba"]), row(lw["lru_bx"]), row(lw["lru_a_logit"]),
        conv_state.transpose(1, 0, 2).reshape(tail * b, lru_w), h0)
    new_conv = new_conv_tm.reshape(tail, b, lru_w).transpose(1, 0, 2)

    qm0 = lru0 + 2 * lru_w
    mem_w = lw["w_br_mem"].shape[0]
    merge_weights = (lw["w_in"][:, qm0:qm0 + mem_w], row(lw["q_norm_g"]), lw["w_in"][:, qm0 + mem_w:],
                     row(lw["b_merge"]), lw["w_br_sb"], lw["w_br_lru"], lw["w_br_mem"], lw["w_out"])
    xn2, o_lru2 = xn.reshape(b * t, d), o_lru.reshape(b * t, lru_w)
    if past_k is None:
        x1 = _attn_merge(x2, xn2, o_lru2, q, k_bf, v_bf, _suffix_sum_matrix(SB_TILE), mem_k.astype(BF16),
                         mem_v.astype(BF16), *merge_weights)
    else:
        o_sb = _sb_sample(q, k_bf, v_bf, past_k, past_v, _suffix_sum_matrix(SB_TILE), _suffix_sum_matrix(t))
        x1 = _merge(x2, xn2, o_sb, o_lru2, mem_k, mem_v, *merge_weights, t)
    y = _ffn(x1, row(lw["norm_ffn_g"]), lw["w_ffn_gate"], lw["w_ffn_up"], lw["w_ffn_down"])
    return y.reshape(b, t, d), k_new, v_new, new_conv, h_last


_BF16_WEIGHTS = ("w_in", "lru_wa", "lru_wx", "w_br_sb", "w_br_lru", "w_br_mem", "w_out", "w_ffn_gate",
                 "w_ffn_up", "w_ffn_down")


def kernel(x_prompt, x_sample, mem_prompt, cache_sb_k, cache_sb_v, state_conv, state_lru_h, cache_mem_k, cache_mem_v, norm_mix_g, w_in, b_merge, conv_w, conv_b, lru_wa, lru_ba, lru_wx, lru_bx, lru_a_logit, q_norm_g, k_norm_g, mem_norm_g, w_mem_k, w_mem_v, w_br_sb, w_br_lru, w_br_mem, w_out, norm_ffn_g, w_ffn_gate, w_ffn_up, w_ffn_down):
    depth = w_in.shape[0]
    bp = x_prompt.shape[0]
    lru_w = conv_w.shape[2]
    hp, hs = x_prompt, x_sample
    outs = [[] for _ in range(10)]
    for l in range(depth):
        lw = {
            "norm_mix_g": norm_mix_g[l], "w_in": w_in[l], "b_merge": b_merge[l], "conv_w": conv_w[l],
            "conv_b": conv_b[l], "lru_wa": lru_wa[l], "lru_ba": lru_ba[l], "lru_wx": lru_wx[l],
            "lru_bx": lru_bx[l], "lru_a_logit": lru_a_logit[l], "q_norm_g": q_norm_g[l],
            "w_br_sb": w_br_sb[l], "w_br_lru": w_br_lru[l], "w_br_mem": w_br_mem[l], "w_out": w_out[l],
            "norm_ffn_g": norm_ffn_g[l], "w_ffn_gate": w_ffn_gate[l], "w_ffn_up": w_ffn_up[l],
            "w_ffn_down": w_ffn_down[l],
        }
        for name in _BF16_WEIGHTS:
            lw[name] = lw[name].astype(BF16)
        mk_p, mv_p = _memory_kv(mem_prompt, mem_norm_g[l].reshape(1, -1), w_mem_k[l].astype(BF16),
                                w_mem_v[l].astype(BF16), k_norm_g[l].reshape(1, -1))
        hp, k_p, v_p, c_p, s_p = _layer(hp, None, None, jnp.zeros((bp, CONV_WIDTH - 1, lru_w), F32),
                                        jnp.zeros((bp, lru_w), F32), mk_p, mv_p, lw)
        hs, k_s, v_s, c_s, s_s = _layer(hs, cache_sb_k[l], cache_sb_v[l], state_conv[l], state_lru_h[l],
                                        cache_mem_k[l], cache_mem_v[l], lw)
        for lst, val in zip(outs, (k_p, v_p, c_p, s_p, mk_p, mv_p, k_s, v_s, c_s, s_s)):
            lst.append(val)
    return (hp, hs) + tuple(lst[0][None] if depth == 1 else jnp.stack(lst) for lst in outs)
```

```python
import functools
import math

import jax
import jax.numpy as jnp
from jax import lax
from jax.experimental import pallas as pl
from jax.experimental.pallas import tpu as pltpu

F32 = jnp.float32
BF16 = jnp.bfloat16

EPS = 1e-6
LOG2E = 1.4426950408889634
SB_HEADS = 8
SB_HEAD_DIM = 128
LRU_BLOCKS = 8
LRU_C = 8.0
CONV_WIDTH = 4
MEM_HEADS = 4
N_BRANCH = 3

V7X_SUBLANES = 8
V7X_LANES = 128
V7X_MXU_DIM = 256
V7X_VMEM_LIMIT_BYTES = 62 * 1024 * 1024

SB_TILE = V7X_MXU_DIM
SB_HEADS_PER_STEP = 4
SB_QUERY_TILES_PER_STEP = 2
SB_EXIT_DROP = 150.0
TOKEN_TILE = 512
MERGE_SHORT_SEQUENCES = 4


def _dot(a, b):
    return jnp.dot(a, b, preferred_element_type=F32)


def _dot_nt(a, b):
    return lax.dot_general(a, b, (((1,), (1,)), ((), ())), preferred_element_type=F32)


def _sigmoid(x):
    return 1.0 / (1.0 + jnp.exp(-x))


def _rms(x, g):
    ms = jnp.mean(x * x, axis=-1, keepdims=True)
    return x * lax.rsqrt(ms + EPS) * g


def _const_spec(shape):
    zeros = (0,) * len(shape)
    return pl.BlockSpec(shape, lambda *_: zeros, pipeline_mode=pl.Buffered(1))


def _params(semantics):
    return pltpu.CompilerParams(dimension_semantics=semantics, vmem_limit_bytes=V7X_VMEM_LIMIT_BYTES)


def _sb_scores(q, k):
    z = _dot_nt(q, k)
    neg, pos = jnp.minimum(z, 0.0), jnp.maximum(z, 0.0)
    tail = jnp.log2(1.0 + jnp.exp2(neg - pos))
    return neg - tail, pos + tail


def _suffix_sums(drop, u):
    return _dot(drop.astype(BF16), u)


def _row_sum(x):
    return jnp.sum(x, axis=-1, keepdims=True)


def _sb_tiles(qs, ks, vs, u, cs, mask):
    scores = [_sb_scores(q, k) for q, k in zip(qs, ks)]
    drops = [drop if mask is None else jnp.where(mask, drop, 0.0) for _, drop in scores]
    sums = [_suffix_sums(drop, u) for drop in drops]
    accs, new_cs = [], []
    for (log_beta, _), drop, s, v, c in zip(scores, drops, sums, vs, cs):
        w = jnp.exp2(log_beta - s - c)
        if mask is not None:
            w = jnp.where(mask, w, 0.0)
        accs.append(_dot(w.astype(BF16), v))
        new_cs.append(c + _row_sum(drop))
    return tuple(accs), tuple(new_cs)


def _sb_own_and_previous(qs, k2s, v2s, u, mask, prev_valid=None, between_stages=(None, None)):
    tk = mask.shape[1]
    prev_valid = prev_valid or [None] * len(qs)
    scores = [_sb_scores(q, k2) for q, k2 in zip(qs, k2s)]
    drops = [(jnp.where(mask, drop[:, tk:], 0.0),
              drop[:, :tk] if ok is None else jnp.where(ok, drop[:, :tk], 0.0))
             for (_, drop), ok in zip(scores, prev_valid)]
    if between_stages[0] is not None:
        between_stages[0]()
    sums = [(_suffix_sums(own, u), _suffix_sums(prev, u)) for own, prev in drops]
    if between_stages[1] is not None:
        between_stages[1]()
    accs, cs = [], []
    for (log_beta, _), (drop_own, drop_prev), (sum_own, sum_prev), v2, ok in zip(scores, drops, sums, v2s,
                                                                                 prev_valid):
        c_own = _row_sum(drop_own)
        w_own = jnp.where(mask, jnp.exp2(log_beta[:, tk:] - sum_own), 0.0)
        w_prev = jnp.exp2(log_beta[:, :tk] - sum_prev - c_own)
        if ok is not None:
            w_prev = jnp.where(ok, w_prev, 0.0)
        w = jnp.concatenate([w_prev, w_own], axis=1).astype(BF16)
        accs.append(_dot(w, v2))
        cs.append(c_own + _row_sum(drop_prev))
    return tuple(accs), tuple(cs)


def _causal_mask(tq, tk):
    row = lax.broadcasted_iota(jnp.int32, (tq, tk), 0)
    col = lax.broadcasted_iota(jnp.int32, (tq, tk), 1)
    return col < row


def _min_all(cs):
    return functools.reduce(jnp.minimum, [jnp.min(c) for c in cs])


def _sb_sweep_rest(first_tile, tiles_fn, accs, cs):
    def cond(state):
        return jnp.logical_and(state[0] >= 0, state[1] < SB_EXIT_DROP)

    def body(state):
        kb, _, accs, cs = state
        new_accs, cs = tiles_fn(kb, cs)
        accs = tuple(acc + new for acc, new in zip(accs, new_accs))
        return kb - 1, _min_all(cs), accs, cs

    state = lax.while_loop(cond, body, (first_tile, _min_all(cs), tuple(accs), tuple(cs)))
    return state[2]


def _fetch_key_tile(k_hbm, v_hbm, seq_index, kb, kbuf, vbuf, sem):
    tk = kbuf.shape[1]
    start = pl.multiple_of(kb * tk, tk)
    copies = [pltpu.make_async_copy(k_hbm.at[seq_index, :, pl.ds(start, tk), :], kbuf, sem.at[0]),
              pltpu.make_async_copy(v_hbm.at[seq_index, :, pl.ds(start, tk), :], vbuf, sem.at[1])]
    for copy in copies:
        copy.start()
    for copy in copies:
        copy.wait()


def _sb_sample_kernel(q_ref, kn_ref, vn_ref, kl_ref, vl_ref, k_hbm, v_hbm, u_ref, us_ref, o_ref, kbuf, vbuf, sem,
                      *, n_past):
    nh, t, hd = q_ref.shape
    u = u_ref[...]
    qs = [q_ref[h] for h in range(nh)]
    acc_new, cs = _sb_tiles(qs, [kn_ref[h] for h in range(nh)], [vn_ref[h] for h in range(nh)], us_ref[...],
                            [0.0] * nh, _causal_mask(t, t))
    acc_past, cs = _sb_tiles(qs, [kl_ref[h].astype(BF16) for h in range(nh)],
                             [vl_ref[h].astype(BF16) for h in range(nh)], u, cs, None)

    def older_tiles(kb, cs):
        _fetch_key_tile(k_hbm, v_hbm, pl.program_id(0), kb, kbuf, vbuf, sem)
        return _sb_tiles(qs, [kbuf[h].astype(BF16) for h in range(nh)], [vbuf[h].astype(BF16) for h in range(nh)],
                         u, cs, None)

    accs = _sb_sweep_rest(n_past - 2, older_tiles, [a + b for a, b in zip(acc_new, acc_past)], cs)
    for h in range(nh):
        o_ref[:, h * hd:(h + 1) * hd] = accs[h].astype(o_ref.dtype)


def _sb_sample(q, k_new, v_new, k_past, v_past, u, u_small):
    b, nh, t, hd = q.shape
    tk = u.shape[1]
    n_past = k_past.shape[2] // tk
    assert k_past.shape[2] % tk == 0
    new_spec = pl.BlockSpec((None, nh, t, hd), lambda bi: (bi, 0, 0, 0))
    last_spec = pl.BlockSpec((None, nh, tk, hd), lambda bi: (bi, 0, n_past - 1, 0))
    any_spec = pl.BlockSpec(memory_space=pl.ANY)
    return pl.pallas_call(
        functools.partial(_sb_sample_kernel, n_past=n_past),
        grid=(b,),
        in_specs=[new_spec, new_spec, new_spec, last_spec, last_spec, any_spec, any_spec, _const_spec(u.shape),
                  _const_spec(u_small.shape)],
        out_specs=pl.BlockSpec((t, nh * hd), lambda bi: (bi, 0)),
        out_shape=jax.ShapeDtypeStruct((b * t, nh * hd), BF16),
        scratch_shapes=[pltpu.VMEM((nh, tk, hd), k_past.dtype), pltpu.VMEM((nh, tk, hd), v_past.dtype),
                        pltpu.SemaphoreType.DMA((2,))],
        compiler_params=_params(("parallel",)),
        name="sb_sample",
    )(q, k_new, v_new, k_past, v_past, k_past, v_past, u, u_small)


def _suffix_sum_matrix(n):
    r = lax.broadcasted_iota(jnp.int32, (n, n), 0)
    c = lax.broadcasted_iota(jnp.int32, (n, n), 1)
    return (r > c).astype(BF16)


def _in_kernel(x_ref, g_ref, wqkv_ref, w_ref, cw_ref, cb_ref, wg_ref, ba_ref, bx_ref, al_ref, cs_ref, h0_ref,
               xn_ref, q_ref, kb_ref, vb_ref, kf_ref, vf_ref, o_ref, nc_ref, hl_ref, xt, ht, tailbuf, hcar):
    batch, steps, d = x_ref.shape
    width = o_ref.shape[2]
    rows = batch * steps
    slabs, _, lanes = xt.shape
    pitch = xt.shape[1] // batch
    groups = batch // V7X_SUBLANES
    tail_rows = (CONV_WIDTH - 1) * batch
    i = pl.program_id(0)
    last = pl.num_programs(0) - 1

    @pl.when(i == 0)
    def _():
        tailbuf[...] = cs_ref[...]
        hcar[...] = h0_ref[...]

    xn = _rms(x_ref[...].reshape(rows, d), g_ref[...]).astype(BF16)
    xn_ref[...] = xn.reshape(batch, steps, d)
    xl = _dot(xn, w_ref[:, :width])
    for j in range(slabs):
        for b in range(batch):
            xt[j, b * pitch:b * pitch + steps, :] = xl[b * steps:(b + 1) * steps, j * lanes:(j + 1) * lanes]

    nh, hd = q_ref.shape[1], q_ref.shape[3]
    q_scale = LOG2E / math.sqrt(hd)
    half = width // 2
    gl_halves = []

    def project_gate_half(k):
        gl_halves.append(_dot(xn, w_ref[:, width + k * half:width + (k + 1) * half]))

    def project_heads(part, k, b_ref, f_ref):
        res = _dot(xn, wqkv_ref[:, part * nh * hd + k * half:part * nh * hd + (k + 1) * half])
        for h in range(half // hd):
            blk = res[:, h * hd:(h + 1) * hd].reshape(batch, steps, hd)
            if f_ref is None:
                blk = blk * q_scale
            else:
                f_ref[:, k * (half // hd) + h] = blk
            b_ref[:, k * (half // hd) + h] = blk.astype(BF16)

    filler = [functools.partial(project_gate_half, 0), functools.partial(project_gate_half, 1)]
    for part, (b_ref, f_ref) in enumerate(((q_ref, None), (kb_ref, kf_ref), (vb_ref, vf_ref))):
        filler += [functools.partial(project_heads, part, k, b_ref, f_ref) for k in range(2)]
    assert len(filler) == slabs

    al = al_ref[...]
    log_sig8 = LRU_C * (jnp.minimum(al, 0.0) - jnp.log(1.0 + jnp.exp(-jnp.abs(al))))
    k1 = -2.0 * 0.7978845608028654 * LOG2E

    assert wg_ref.shape[0] == slabs and wg_ref.shape[1] == lanes
    for j in range(slabs):
        cols = slice(j * lanes, (j + 1) * lanes)
        gathered = [jnp.concatenate([xt[j, pl.ds(g * V7X_SUBLANES * pitch + s, V7X_SUBLANES, stride=pitch), :]
                                     for g in range(groups)], axis=0) for s in range(steps)]
        xl_tm = jnp.concatenate([tailbuf[:, cols]] + gathered, axis=0)
        tailbuf[:, cols] = xl_tm[rows:rows + tail_rows, :]
        xc = cb_ref[:, cols]
        for tap in range(CONV_WIDTH):
            xc = xc + xl_tm[tap * batch:tap * batch + rows, :] * cw_ref[tap:tap + 1, cols]
        gates = _dot(xc.astype(BF16), wg_ref[j])
        filler[j]()

        r = _sigmoid(gates[:, :lanes] + ba_ref[:, cols])
        ig = _sigmoid(gates[:, lanes:] + bx_ref[:, cols])
        log_a = r * log_sig8[:, cols]
        a = jnp.exp(log_a)
        th = jnp.tanh(log_a)
        m2 = -2.0 * th / (1.0 - th)
        bterm = jnp.where(m2 > 0.0, m2 * lax.rsqrt(m2), 0.0) * (ig * xc)

        h = hcar[:, cols]
        for s in range(steps):
            h = a[s * batch:(s + 1) * batch] * h + bterm[s * batch:(s + 1) * batch]
            for g in range(groups):
                ht[j, pl.ds(g * V7X_SUBLANES * pitch + s, V7X_SUBLANES, stride=pitch), :] = (
                    h[g * V7X_SUBLANES:(g + 1) * V7X_SUBLANES])
        hcar[:, cols] = h

        hs = jnp.concatenate([ht[j, b * pitch:b * pitch + steps, :] for b in range(batch)], axis=0)
        off = (j * lanes) % half
        glj = gl_halves[j * lanes // half][:, off:off + lanes]
        gate = glj / (1.0 + jnp.exp2(glj * (k1 + (k1 * 0.044715) * (glj * glj))))
        o_ref[:, :, cols] = (hs * gate).astype(o_ref.dtype).reshape(batch, steps, lanes)

    @pl.when(i == last)
    def _():
        nc_ref[...] = tailbuf[...]
        hl_ref[...] = hcar[...]


def _in_proj(x, g, w_qkv, w_lru, conv_w, conv_b, w_gates, ba, bx, a_logit, conv_state_tm, h0):
    batch, seq, d = x.shape
    nh, hd = SB_HEADS, SB_HEAD_DIM
    width = w_lru.shape[1] // 2
    tail_rows = (CONV_WIDTH - 1) * batch
    assert batch % V7X_SUBLANES == 0
    steps = max(1, min(seq, TOKEN_TILE // batch))
    pitch = steps + V7X_SUBLANES if (steps // V7X_SUBLANES) % 2 == 0 else steps
    slab = (width // V7X_LANES, batch * pitch, V7X_LANES)
    tok_of = lambda w: pl.BlockSpec((batch, steps, w), lambda i: (0, i, 0))
    head_spec = pl.BlockSpec((batch, nh, steps, hd), lambda i: (0, 0, i, 0))
    head_bf = jax.ShapeDtypeStruct((batch, nh, seq, hd), BF16)
    head_f32 = jax.ShapeDtypeStruct((batch, nh, seq, hd), F32)
    return pl.pallas_call(
        _in_kernel,
        grid=(seq // steps,),
        in_specs=[tok_of(d), _const_spec((1, d)), _const_spec(w_qkv.shape), _const_spec(w_lru.shape),
                  _const_spec(conv_w.shape), _const_spec((1, width)), _const_spec(w_gates.shape),
                  _const_spec((1, width)), _const_spec((1, width)), _const_spec((1, width)),
                  _const_spec((tail_rows, width)), _const_spec((batch, width))],
        out_specs=[tok_of(d), head_spec, head_spec, head_spec, head_spec, head_spec, tok_of(width),
                   pl.BlockSpec((tail_rows, width), lambda i: (0, 0)),
                   pl.BlockSpec((batch, width), lambda i: (0, 0))],
        out_shape=[jax.ShapeDtypeStruct((batch, seq, d), BF16), head_bf, head_bf, head_bf, head_f32, head_f32,
                   jax.ShapeDtypeStruct((batch, seq, width), BF16),
                   jax.ShapeDtypeStruct((tail_rows, width), F32),
                   jax.ShapeDtypeStruct((batch, width), F32)],
        scratch_shapes=[pltpu.VMEM(slab, F32), pltpu.VMEM(slab, F32), pltpu.VMEM((tail_rows, width), F32),
                        pltpu.VMEM((batch, width), F32)],
        compiler_params=_params(("arbitrary",)),
        name="in_proj",
    )(x, g, w_qkv, w_lru, conv_w, conv_b, w_gates, ba, bx, a_logit, conv_state_tm, h0)


def _memkv_kernel(mem_ref, g_ref, wk_ref, wv_ref, kg_ref, k_ref, v_ref):
    nh, n, hd = k_ref.shape
    m = _rms(mem_ref[...], g_ref[...]).astype(BF16)
    k = _dot(m, wk_ref[...])
    v = _dot(m, wv_ref[...])
    for h in range(nh):
        k_ref[h] = _rms(k[:, h * hd:(h + 1) * hd], kg_ref[...])
        v_ref[h] = v[:, h * hd:(h + 1) * hd]


def _memory_kv(mem, g, wk, wv, kg):
    b, n, d = mem.shape
    nh = MEM_HEADS
    hd = wk.shape[1] // nh
    out = jax.ShapeDtypeStruct((b, nh, n, hd), F32)
    out_spec = pl.BlockSpec((None, nh, n, hd), lambda i: (i, 0, 0, 0))
    return pl.pallas_call(
        _memkv_kernel,
        grid=(b,),
        in_specs=[pl.BlockSpec((None, n, d), lambda i: (i, 0, 0)), _const_spec((1, d)), _const_spec(wk.shape),
                  _const_spec(wv.shape), _const_spec((1, hd))],
        out_specs=[out_spec, out_spec],
        out_shape=[out, out],
        compiler_params=_params(("parallel",)),
        name="memory_kv",
    )(mem, g, wk, wv, kg)


def _memory_attention(qm, mk_ref, mv_ref, j, qg):
    _, nh, _, hd = mk_ref.shape
    heads = []
    for h in range(nh):
        qh = _rms(qm[:, h * hd:(h + 1) * hd], qg).astype(BF16)
        s = _dot_nt(qh, mk_ref[j, h].astype(BF16)) / math.sqrt(hd)
        e = jnp.exp(s - jnp.max(s, axis=-1, keepdims=True))
        p = e / jnp.sum(e, axis=-1, keepdims=True)
        heads.append(_dot(p.astype(BF16), mv_ref[j, h].astype(BF16)))
    return jnp.concatenate(heads, axis=1)


def _merge_kernel(x_ref, xn_ref, osb_ref, olru_ref, mk_ref, mv_ref, wqm_ref, qg_ref, wg_ref, bm_ref,
                  wsb_ref, wlru_ref, wmem_ref, wout_ref, o_ref):
    nb = mk_ref.shape[0]
    tm, d = o_ref.shape
    per_seq = tm // nb
    xn = xn_ref[...]

    qm = _dot(xn, wqm_ref[...])
    omem = jnp.concatenate([_memory_attention(qm[j * per_seq:(j + 1) * per_seq], mk_ref, mv_ref, j, qg_ref[...])
                            for j in range(nb)], axis=0).astype(BF16)

    gates = _sigmoid(_dot(xn, wg_ref[...]) + bm_ref[...])
    merged = (gates[:, :d] * _dot(osb_ref[...], wsb_ref[...])
              + gates[:, d:2 * d] * _dot(olru_ref[...], wlru_ref[...])
              + gates[:, 2 * d:] * _dot(omem, wmem_ref[...]))
    o_ref[...] = x_ref[...] + _dot(merged.astype(BF16), wout_ref[...])


def _attn_merge_kernel(x_ref, xn_ref, olru_ref, q_ref, *refs, tiles_per_seq):
    nh, q_rows, hd = q_ref.shape
    nt = (len(refs) - 19) // 2
    k_tiles, v_tiles = refs[:nt + 1], refs[nt + 1:2 * nt + 2]
    (k_hbm, v_hbm, u_ref, mk_ref, mv_ref, wqm_ref, qg_ref, wg_ref, bm_ref, wsb_ref, wlru_ref, wmem_ref, wout_ref,
     o_ref, kbuf, vbuf, sem) = refs[2 * nt + 2:]
    tk = u_ref.shape[1]
    tq = tk
    assert nt == q_rows // tq
    d = o_ref.shape[1]
    seq_index = pl.program_id(0) // tiles_per_seq
    first = (pl.program_id(0) % tiles_per_seq) * nt
    u = u_ref[...]
    mask = _causal_mask(tq, tk)
    units = [(h, t) for t in range(nt) for h in range(nh)]
    qs = [q_ref[h, t * tq:(t + 1) * tq, :] for h, t in units]
    k2s = [jnp.concatenate([k_tiles[t][h], k_tiles[t + 1][h]], axis=0) for h, t in units]
    v2s = [jnp.concatenate([v_tiles[t][h], v_tiles[t + 1][h]], axis=0) for h, t in units]
    has_previous = first > 0
    prev_valid = [has_previous if t == 0 else None for _, t in units]

    xn = xn_ref[...]
    bm = bm_ref[...]
    early = {}

    def gate(n):
        return _sigmoid(_dot(xn, wg_ref[:, n * d:(n + 1) * d]) + bm[:, n * d:(n + 1) * d])

    def project_memory_queries():
        early["qm"] = _dot(xn, wqm_ref[...])

    def attend_memory():
        early["omem"] = _memory_attention(early.pop("qm"), mk_ref, mv_ref, 0, qg_ref[...]).astype(BF16)

    def recurrent_gate():
        early["g1"] = gate(1)

    def recurrent_term():
        early["rest"] = early.pop("g1") * _dot(olru_ref[...], wlru_ref[...])

    def memory_gate():
        early["g2"] = gate(2)

    def memory_term():
        early["rest"] = early["rest"] + early.pop("g2") * _dot(early.pop("omem"), wmem_ref[...])

    def sweep_gate():
        early["g0"] = gate(0)

    gaps = [project_memory_queries, attend_memory, recurrent_gate, recurrent_term, memory_gate, memory_term,
            sweep_gate]
    gaps += [None] * (2 * len(units) // SB_HEADS_PER_STEP - len(gaps))
    accs, cs = (), ()
    for start in range(0, len(units), SB_HEADS_PER_STEP):
        sel = slice(start, start + SB_HEADS_PER_STEP)
        acc_g, c_g = _sb_own_and_previous(qs[sel], k2s[sel], v2s[sel], u, mask, prev_valid[sel],
                                          (gaps.pop(0), gaps.pop(0)))
        accs, cs = accs + acc_g, cs + c_g
    assert not gaps

    def sweep_rest():
        out = []
        for t in range(nt):
            sel = slice(t * nh, (t + 1) * nh)

            def tiles_fn(kb, cs_t, sel=sel):
                _fetch_key_tile(k_hbm, v_hbm, seq_index, kb, kbuf, vbuf, sem)
                return _sb_tiles(qs[sel], [kbuf[h] for h in range(nh)], [vbuf[h] for h in range(nh)], u, cs_t,
                                 None)

            out.extend(_sb_sweep_rest(first + t - 2, tiles_fn, accs[sel], cs[sel]))
        return tuple(out)

    accs = lax.cond(_min_all(cs) < SB_EXIT_DROP, sweep_rest, lambda: accs)
    osb = jnp.concatenate([jnp.concatenate(accs[t * nh:(t + 1) * nh], axis=1) for t in range(nt)],
                          axis=0).astype(BF16)

    merged = early["g0"] * _dot(osb, wsb_ref[...]) + early["rest"]
    o_ref[...] = x_ref[...] + _dot(merged.astype(BF16), wout_ref[...])


def _attn_merge(x2, xn, olru, q, k, v, u, mk, mv, wqm, qg, wg, bm, wsb, wlru, wmem, wout):
    m, d = x2.shape
    batch, nh, seq, hd = q.shape
    _, nmh, n_mem, mhd = mk.shape
    tk = u.shape[1]
    nt = SB_QUERY_TILES_PER_STEP
    tm = nt * tk
    tiles_per_seq = seq // tm
    assert seq % tm == 0
    tok_of = lambda width: pl.BlockSpec((tm, width), lambda i: (i, 0))
    q_spec = pl.BlockSpec((None, nh, tm, hd), lambda i: (i // tiles_per_seq, 0, i % tiles_per_seq, 0))

    def key_tile(offset):
        return pl.BlockSpec((None, nh, tk, hd), lambda i: (i // tiles_per_seq, 0,
                                                           jnp.maximum((i % tiles_per_seq) * nt + offset, 0), 0))

    any_spec = pl.BlockSpec(memory_space=pl.ANY)
    mem_spec = pl.BlockSpec((1, nmh, n_mem, mhd), lambda i: (i // tiles_per_seq, 0, 0, 0))
    key_tiles = [key_tile(offset) for offset in range(-1, nt)]
    return pl.pallas_call(
        functools.partial(_attn_merge_kernel, tiles_per_seq=tiles_per_seq),
        grid=(m // tm,),
        in_specs=[tok_of(d), tok_of(d), tok_of(olru.shape[1]), q_spec, *key_tiles, *key_tiles, any_spec,
                  any_spec, _const_spec(u.shape), mem_spec,
                  mem_spec, _const_spec(wqm.shape), _const_spec((1, mhd)), _const_spec(wg.shape),
                  _const_spec((1, N_BRANCH * d)), _const_spec(wsb.shape), _const_spec(wlru.shape),
                  _const_spec(wmem.shape), _const_spec(wout.shape)],
        out_specs=tok_of(d),
        out_shape=jax.ShapeDtypeStruct((m, d), F32),
        scratch_shapes=[pltpu.VMEM((nh, tk, hd), BF16), pltpu.VMEM((nh, tk, hd), BF16),
                        pltpu.SemaphoreType.DMA((2,))],
        compiler_params=_params(("parallel",)),
        name="attn_merge",
    )(x2, xn, olru, q, *([k] * (nt + 1)), *([v] * (nt + 1)), k, v, u, mk, mv, wqm, qg, wg, bm, wsb, wlru, wmem,
      wout)


def _merge(x2, xn, osb, olru, mk, mv, wqm, qg, wg, bm, wsb, wlru, wmem, wout, seq):
    m, d = x2.shape
    _, nh, n_mem, hd = mk.shape
    if seq >= TOKEN_TILE:
        tm, nb = TOKEN_TILE, 1
        per_batch = seq // tm
        mem_spec = pl.BlockSpec((nb, nh, n_mem, hd), lambda i: (i // per_batch, 0, 0, 0))
    else:
        nb = MERGE_SHORT_SEQUENCES
        tm = nb * seq
        mem_spec = pl.BlockSpec((nb, nh, n_mem, hd), lambda i: (i, 0, 0, 0))
    tok_of = lambda width: pl.BlockSpec((tm, width), lambda i: (i, 0))
    return pl.pallas_call(
        _merge_kernel,
        grid=(m // tm,),
        in_specs=[tok_of(d), tok_of(d), tok_of(osb.shape[1]), tok_of(olru.shape[1]), mem_spec, mem_spec,
                  _const_spec(wqm.shape), _const_spec((1, hd)), _const_spec(wg.shape),
                  _const_spec((1, N_BRANCH * d)), _const_spec(wsb.shape), _const_spec(wlru.shape),
                  _const_spec(wmem.shape), _const_spec(wout.shape)],
        out_specs=tok_of(d),
        out_shape=jax.ShapeDtypeStruct((m, d), F32),
        compiler_params=_params(("parallel",)),
        name="merge",
    )(x2, xn, osb, olru, mk, mv, wqm, qg, wg, bm, wsb, wlru, wmem, wout)


def _ffn_kernel(x_ref, g_ref, wgate_ref, wup_ref, wdown_ref, o_ref):
    x = x_ref[...]
    xn = _rms(x, g_ref[...]).astype(BF16)
    gate = _dot(xn, wgate_ref[...])
    up = _dot(xn, wup_ref[...])
    hidden = (gate * _sigmoid(gate) * up).astype(BF16)
    o_ref[...] = x + _dot(hidden, wdown_ref[...])


def _ffn(x1, g, wgate, wup, wdown):
    m, d = x1.shape
    tm = min(TOKEN_TILE, m)
    tok = pl.BlockSpec((tm, d), lambda i: (i, 0))
    return pl.pallas_call(
        _ffn_kernel,
        grid=(m // tm,),
        in_specs=[tok, _const_spec((1, d)), _const_spec(wgate.shape), _const_spec(wup.shape),
                  _const_spec(wdown.shape)],
        out_specs=tok,
        out_shape=jax.ShapeDtypeStruct((m, d), F32),
        compiler_params=_params(("parallel",)),
        name="ffn",
    )(x1, g, wgate, wup, wdown)


def _layer(x, past_k, past_v, conv_state, h0, mem_k, mem_v, lw):
    b, t, d = x.shape
    x2 = x.reshape(b * t, d)
    sbw = SB_HEADS * SB_HEAD_DIM
    lru_w = lw["conv_w"].shape[1]
    tail = CONV_WIDTH - 1
    row = lambda v: v.reshape(1, -1)

    lru0 = 3 * sbw
    w_gates = jnp.concatenate([lw["lru_wa"], lw["lru_wx"]], axis=2)
    xn, q, k_bf, v_bf, k_new, v_new, o_lru, new_conv_tm, h_last = _in_proj(
        x, row(lw["norm_mix_g"]), lw["w_in"][:, :lru0], lw["w_in"][:, lru0:lru0 + 2 * lru_w], lw["conv_w"],
        row(lw["conv_b"]), w_gates, row(lw["lru_ba"]), row(lw["lru_bx"]), row(lw["lru_a_logit"]),
        conv_state.transpose(1, 0, 2).reshape(tail * b, lru_w), h0)
    new_conv = new_conv_tm.reshape(tail, b, lru_w).transpose(1, 0, 2)

    qm0 = lru0 + 2 * lru_w
    mem_w = lw["w_br_mem"].shape[0]
    merge_weights = (lw["w_in"][:, qm0:qm0 + mem_w], row(lw["q_norm_g"]), lw["w_in"][:, qm0 + mem_w:],
                     row(lw["b_merge"]), lw["w_br_sb"], lw["w_br_lru"], lw["w_br_mem"], lw["w_out"])
    xn2, o_lru2 = xn.reshape(b * t, d), o_lru.reshape(b * t, lru_w)
    if past_k is None:
        x1 = _attn_merge(x2, xn2, o_lru2, q, k_bf, v_bf, _suffix_sum_matrix(SB_TILE), mem_k.astype(BF16),
                         mem_v.astype(BF16), *merge_weights)
    else:
        o_sb = _sb_sample(q, k_bf, v_bf, past_k, past_v, _suffix_sum_matrix(SB_TILE), _suffix_sum_matrix(t))
        x1 = _merge(x2, xn2, o_sb, o_lru2, mem_k, mem_v, *merge_weights, t)
    y = _ffn(x1, row(lw["norm_ffn_g"]), lw["w_ffn_gate"], lw["w_ffn_up"], lw["w_ffn_down"])
    return y.reshape(b, t, d), k_new, v_new, new_conv, h_last


_BF16_WEIGHTS = ("w_in", "lru_wa", "lru_wx", "w_br_sb", "w_br_lru", "w_br_mem", "w_out", "w_ffn_gate",
                 "w_ffn_up", "w_ffn_down")


def kernel(x_prompt, x_sample, mem_prompt, cache_sb_k, cache_sb_v, state_conv, state_lru_h, cache_mem_k, cache_mem_v, norm_mix_g, w_in, b_merge, conv_w, conv_b, lru_wa, lru_ba, lru_wx, lru_bx, lru_a_logit, q_norm_g, k_norm_g, mem_norm_g, w_mem_k, w_mem_v, w_br_sb, w_br_lru, w_br_mem, w_out, norm_ffn_g, w_ffn_gate, w_ffn_up, w_ffn_down):
    depth = w_in.shape[0]
    bp = x_prompt.shape[0]
    lru_w = conv_w.shape[2]
    hp, hs = x_prompt, x_sample
    outs = [[] for _ in range(10)]
    for l in range(depth):
        lw = {
            "norm_mix_g": norm_mix_g[l], "w_in": w_in[l], "b_merge": b_merge[l], "conv_w": conv_w[l],
            "conv_b": conv_b[l], "lru_wa": lru_wa[l], "lru_ba": lru_ba[l], "lru_wx": lru_wx[l],
            "lru_bx": lru_bx[l], "lru_a_logit": lru_a_logit[l], "q_norm_g": q_norm_g[l],
            "w_br_sb": w_br_sb[l], "w_br_lru": w_br_lru[l], "w_br_mem": w_br_mem[l], "w_out": w_out[l],
            "norm_ffn_g": norm_ffn_g[l], "w_ffn_gate": w_ffn_gate[l], "w_ffn_up": w_ffn_up[l],
            "w_ffn_down": w_ffn_down[l],
        }
        for name in _BF16_WEIGHTS:
            lw[name] = lw[name].astype(BF16)
        mk_p, mv_p = _memory_kv(mem_prompt, mem_norm_g[l].reshape(1, -1), w_mem_k[l].astype(BF16),
                                w_mem_v[l].astype(BF16), k_norm_g[l].reshape(1, -1))
        hp, k_p, v_p, c_p, s_p = _layer(hp, None, None, jnp.zeros((bp, CONV_WIDTH - 1, lru_w), F32),
                                        jnp.zeros((bp, lru_w), F32), mk_p, mv_p, lw)
        hs, k_s, v_s, c_s, s_s = _layer(hs, cache_sb_k[l], cache_sb_v[l], state_conv[l], state_lru_h[l],
                                        cache_mem_k[l], cache_mem_v[l], lw)
        for lst, val in zip(outs, (k_p, v_p, c_p, s_p, mk_p, mv_p, k_s, v_s, c_s, s_s)):
            lst.append(val)
    return (hp, hs) + tuple(lst[0][None] if depth == 1 else jnp.stack(lst) for lst in outs)
```

```python
import functools
import math

import jax
import jax.numpy as jnp
from jax import lax
from jax.experimental import pallas as pl
from jax.experimental.pallas import tpu as pltpu

F32 = jnp.float32
BF16 = jnp.bfloat16

EPS = 1e-6
LOG2E = 1.4426950408889634
SB_HEADS = 8
SB_HEAD_DIM = 128
LRU_C = 8.0
CONV_WIDTH = 4
MEM_HEADS = 4
N_BRANCH = 3

V7X_SUBLANES = 8
V7X_LANES = 128
V7X_MXU_DIM = 256
V7X_VMEM_LIMIT_BYTES = 62 * 1024 * 1024

SB_TILE = V7X_MXU_DIM
SB_HEADS_PER_STEP = 4
SB_QUERY_TILES_PER_STEP = 2
SB_EXIT_DROP = 150.0
TOKEN_TILE = 512
FFN_TILES_PER_STEP = 2
MERGE_SHORT_SEQUENCES = 4


def _dot(a, b):
    return jnp.dot(a, b, preferred_element_type=F32)


def _dot_nt(a, b):
    return lax.dot_general(a, b, (((1,), (1,)), ((), ())), preferred_element_type=F32)


def _sigmoid(x):
    return 1.0 / (1.0 + jnp.exp(-x))


def _rms(x, g):
    ms = jnp.mean(x * x, axis=-1, keepdims=True)
    return x * lax.rsqrt(ms + EPS) * g


def _const_spec(shape):
    zeros = (0,) * len(shape)
    return pl.BlockSpec(shape, lambda *_: zeros, pipeline_mode=pl.Buffered(1))


def _params(semantics):
    return pltpu.CompilerParams(dimension_semantics=semantics, vmem_limit_bytes=V7X_VMEM_LIMIT_BYTES)


def _sb_scores(q, k):
    z = _dot_nt(q, k)
    neg, pos = jnp.minimum(z, 0.0), jnp.maximum(z, 0.0)
    tail = jnp.log2(1.0 + jnp.exp2(neg - pos))
    return neg - tail, pos + tail


def _suffix_sums(drop, u):
    return _dot(drop.astype(BF16), u)


def _row_sum(x):
    return jnp.sum(x, axis=-1, keepdims=True)


def _sb_tiles(qs, ks, vs, u, cs, mask):
    scores = [_sb_scores(q, k) for q, k in zip(qs, ks)]
    drops = [drop if mask is None else jnp.where(mask, drop, 0.0) for _, drop in scores]
    sums = [_suffix_sums(drop, u) for drop in drops]
    accs, new_cs = [], []
    for (log_beta, _), drop, s, v, c in zip(scores, drops, sums, vs, cs):
        w = jnp.exp2(log_beta - s - c)
        if mask is not None:
            w = jnp.where(mask, w, 0.0)
        accs.append(_dot(w.astype(BF16), v))
        new_cs.append(c + _row_sum(drop))
    return tuple(accs), tuple(new_cs)


def _sb_own_and_previous(qs, k2s, v2s, u, mask, prev_valid=None, between_stages=(None, None)):
    tk = mask.shape[1]
    prev_valid = prev_valid or [None] * len(qs)
    scores = [_sb_scores(q, k2) for q, k2 in zip(qs, k2s)]
    drops = [(jnp.where(mask, drop[:, tk:], 0.0),
              drop[:, :tk] if ok is None else jnp.where(ok, drop[:, :tk], 0.0))
             for (_, drop), ok in zip(scores, prev_valid)]
    if between_stages[0] is not None:
        between_stages[0]()
    sums = [(_suffix_sums(own, u), _suffix_sums(prev, u)) for own, prev in drops]
    if between_stages[1] is not None:
        between_stages[1]()
    accs, cs = [], []
    for (log_beta, _), (drop_own, drop_prev), (sum_own, sum_prev), v2, ok in zip(scores, drops, sums, v2s,
                                                                                 prev_valid):
        c_own = _row_sum(drop_own)
        w_own = jnp.where(mask, jnp.exp2(log_beta[:, tk:] - sum_own), 0.0)
        w_prev = jnp.exp2(log_beta[:, :tk] - sum_prev - c_own)
        if ok is not None:
            w_prev = jnp.where(ok, w_prev, 0.0)
        w = jnp.concatenate([w_prev, w_own], axis=1).astype(BF16)
        accs.append(_dot(w, v2))
        cs.append(c_own + _row_sum(drop_prev))
    return tuple(accs), tuple(cs)


def _causal_mask(tq, tk):
    row = lax.broadcasted_iota(jnp.int32, (tq, tk), 0)
    col = lax.broadcasted_iota(jnp.int32, (tq, tk), 1)
    return col < row


def _min_all(cs):
    return functools.reduce(jnp.minimum, [jnp.min(c) for c in cs])


def _sb_sweep_rest(first_tile, tiles_fn, accs, cs):
    def cond(state):
        return jnp.logical_and(state[0] >= 0, state[1] < SB_EXIT_DROP)

    def body(state):
        kb, _, accs, cs = state
        new_accs, cs = tiles_fn(kb, cs)
        accs = tuple(acc + new for acc, new in zip(accs, new_accs))
        return kb - 1, _min_all(cs), accs, cs

    state = lax.while_loop(cond, body, (first_tile, _min_all(cs), tuple(accs), tuple(cs)))
    return state[2]


def _fetch_key_tile(k_hbm, v_hbm, seq_index, kb, kbuf, vbuf, sem):
    tk = kbuf.shape[1]
    start = pl.multiple_of(kb * tk, tk)
    copies = [pltpu.make_async_copy(k_hbm.at[seq_index, :, pl.ds(start, tk), :], kbuf, sem.at[0]),
              pltpu.make_async_copy(v_hbm.at[seq_index, :, pl.ds(start, tk), :], vbuf, sem.at[1])]
    for copy in copies:
        copy.start()
    for copy in copies:
        copy.wait()


def _sb_sample_kernel(q_ref, kn_ref, vn_ref, kl_ref, vl_ref, k_hbm, v_hbm, u_ref, us_ref, o_ref, kbuf, vbuf, sem,
                      *, n_past):
    nh, t, hd = q_ref.shape
    u = u_ref[...]
    qs = [q_ref[h] for h in range(nh)]
    acc_new, cs = _sb_tiles(qs, [kn_ref[h] for h in range(nh)], [vn_ref[h] for h in range(nh)], us_ref[...],
                            [0.0] * nh, _causal_mask(t, t))
    acc_past, cs = _sb_tiles(qs, [kl_ref[h].astype(BF16) for h in range(nh)],
                             [vl_ref[h].astype(BF16) for h in range(nh)], u, cs, None)

    def older_tiles(kb, cs):
        _fetch_key_tile(k_hbm, v_hbm, pl.program_id(0), kb, kbuf, vbuf, sem)
        return _sb_tiles(qs, [kbuf[h].astype(BF16) for h in range(nh)], [vbuf[h].astype(BF16) for h in range(nh)],
                         u, cs, None)

    accs = _sb_sweep_rest(n_past - 2, older_tiles, [a + b for a, b in zip(acc_new, acc_past)], cs)
    for h in range(nh):
        o_ref[:, h * hd:(h + 1) * hd] = accs[h].astype(o_ref.dtype)


def _sb_sample(q, k_new, v_new, k_past, v_past, u, u_small):
    b, nh, t, hd = q.shape
    tk = u.shape[1]
    n_past = k_past.shape[2] // tk
    assert k_past.shape[2] % tk == 0
    new_spec = pl.BlockSpec((None, nh, t, hd), lambda bi: (bi, 0, 0, 0))
    last_spec = pl.BlockSpec((None, nh, tk, hd), lambda bi: (bi, 0, n_past - 1, 0))
    any_spec = pl.BlockSpec(memory_space=pl.ANY)
    return pl.pallas_call(
        functools.partial(_sb_sample_kernel, n_past=n_past),
        grid=(b,),
        in_specs=[new_spec, new_spec, new_spec, last_spec, last_spec, any_spec, any_spec, _const_spec(u.shape),
                  _const_spec(u_small.shape)],
        out_specs=pl.BlockSpec((t, nh * hd), lambda bi: (bi, 0)),
        out_shape=jax.ShapeDtypeStruct((b * t, nh * hd), BF16),
        scratch_shapes=[pltpu.VMEM((nh, tk, hd), k_past.dtype), pltpu.VMEM((nh, tk, hd), v_past.dtype),
                        pltpu.SemaphoreType.DMA((2,))],
        compiler_params=_params(("parallel",)),
        name="sb_sample",
    )(q, k_new, v_new, k_past, v_past, k_past, v_past, u, u_small)


def _suffix_sum_matrix(n):
    r = lax.broadcasted_iota(jnp.int32, (n, n), 0)
    c = lax.broadcasted_iota(jnp.int32, (n, n), 1)
    return (r > c).astype(BF16)


def _in_kernel(x_ref, g_ref, wqkv_ref, w_ref, cw_ref, cb_ref, wg_ref, ba_ref, bx_ref, al_ref, cs_ref, h0_ref,
               xn_ref, q_ref, kb_ref, vb_ref, kf_ref, vf_ref, o_ref, nc_ref, hl_ref, xt, ht, tailbuf, hcar):
    batch, steps, d = x_ref.shape
    width = o_ref.shape[2]
    rows = batch * steps
    slabs, _, lanes = xt.shape
    pitch = xt.shape[1] // batch
    groups = batch // V7X_SUBLANES
    tail_rows = (CONV_WIDTH - 1) * batch
    i = pl.program_id(0)
    last = pl.num_programs(0) - 1

    @pl.when(i == 0)
    def _():
        tailbuf[...] = cs_ref[...]
        hcar[...] = h0_ref[...]

    xn = _rms(x_ref[...].reshape(rows, d), g_ref[...]).astype(BF16)
    xn_ref[...] = xn.reshape(batch, steps, d)
    xl = _dot(xn, w_ref[:, :width])
    for j in range(slabs):
        for b in range(batch):
            xt[j, b * pitch:b * pitch + steps, :] = xl[b * steps:(b + 1) * steps, j * lanes:(j + 1) * lanes]

    nh, hd = q_ref.shape[1], q_ref.shape[3]
    q_scale = LOG2E / math.sqrt(hd)
    half = width // 2
    gl_halves = []

    def project_gate_half(k):
        gl_halves.append(_dot(xn, w_ref[:, width + k * half:width + (k + 1) * half]))

    def project_heads(part, k, b_ref, f_ref):
        res = _dot(xn, wqkv_ref[:, part * nh * hd + k * half:part * nh * hd + (k + 1) * half])
        for h in range(half // hd):
            blk = res[:, h * hd:(h + 1) * hd].reshape(batch, steps, hd)
            if f_ref is None:
                blk = blk * q_scale
            else:
                f_ref[:, k * (half // hd) + h] = blk
            b_ref[:, k * (half // hd) + h] = blk.astype(BF16)

    filler = [functools.partial(project_gate_half, 0), functools.partial(project_gate_half, 1)]
    for part, (b_ref, f_ref) in enumerate(((q_ref, None), (kb_ref, kf_ref), (vb_ref, vf_ref))):
        filler += [functools.partial(project_heads, part, k, b_ref, f_ref) for k in range(2)]
    assert len(filler) == slabs

    al = al_ref[...]
    log_sig8 = LRU_C * (jnp.minimum(al, 0.0) - jnp.log(1.0 + jnp.exp(-jnp.abs(al))))
    k1 = -2.0 * 0.7978845608028654 * LOG2E

    assert wg_ref.shape[0] == slabs and wg_ref.shape[1] == lanes
    for j in range(slabs):
        cols = slice(j * lanes, (j + 1) * lanes)
        gathered = [jnp.concatenate([xt[j, pl.ds(g * V7X_SUBLANES * pitch + s, V7X_SUBLANES, stride=pitch), :]
                                     for g in range(groups)], axis=0) for s in range(steps)]
        xl_tm = jnp.concatenate([tailbuf[:, cols]] + gathered, axis=0)
        tailbuf[:, cols] = xl_tm[rows:rows + tail_rows, :]
        xc = cb_ref[:, cols]
        for tap in range(CONV_WIDTH):
            xc = xc + xl_tm[tap * batch:tap * batch + rows, :] * cw_ref[tap:tap + 1, cols]
        gates = _dot(xc.astype(BF16), wg_ref[j])
        filler[j]()

        r = _sigmoid(gates[:, :lanes] + ba_ref[:, cols])
        ig = _sigmoid(gates[:, lanes:] + bx_ref[:, cols])
        log_a = r * log_sig8[:, cols]
        a = jnp.exp(log_a)
        th = jnp.tanh(log_a)
        m2 = -2.0 * th / (1.0 - th)
        bterm = jnp.where(m2 > 0.0, m2 * lax.rsqrt(m2), 0.0) * (ig * xc)

        h = hcar[:, cols]
        for s in range(steps):
            h = a[s * batch:(s + 1) * batch] * h + bterm[s * batch:(s + 1) * batch]
            for g in range(groups):
                ht[j, pl.ds(g * V7X_SUBLANES * pitch + s, V7X_SUBLANES, stride=pitch), :] = (
                    h[g * V7X_SUBLANES:(g + 1) * V7X_SUBLANES])
        hcar[:, cols] = h

        hs = jnp.concatenate([ht[j, b * pitch:b * pitch + steps, :] for b in range(batch)], axis=0)
        off = (j * lanes) % half
        glj = gl_halves[j * lanes // half][:, off:off + lanes]
        gate = glj / (1.0 + jnp.exp2(glj * (k1 + (k1 * 0.044715) * (glj * glj))))
        o_ref[:, :, cols] = (hs * gate).astype(o_ref.dtype).reshape(batch, steps, lanes)

    @pl.when(i == last)
    def _():
        nc_ref[...] = tailbuf[...]
        hl_ref[...] = hcar[...]


def _in_proj(x, g, w_qkv, w_lru, conv_w, conv_b, w_gates, ba, bx, a_logit, conv_state_tm, h0):
    batch, seq, d = x.shape
    nh, hd = SB_HEADS, SB_HEAD_DIM
    width = w_lru.shape[1] // 2
    tail_rows = (CONV_WIDTH - 1) * batch
    assert batch % V7X_SUBLANES == 0
    steps = max(1, min(seq, TOKEN_TILE // batch))
    pitch = steps + V7X_SUBLANES if (steps // V7X_SUBLANES) % 2 == 0 else steps
    slab = (width // V7X_LANES, batch * pitch, V7X_LANES)
    tok_of = lambda w: pl.BlockSpec((batch, steps, w), lambda i: (0, i, 0))
    head_spec = pl.BlockSpec((batch, nh, steps, hd), lambda i: (0, 0, i, 0))
    head_bf = jax.ShapeDtypeStruct((batch, nh, seq, hd), BF16)
    head_f32 = jax.ShapeDtypeStruct((batch, nh, seq, hd), F32)
    return pl.pallas_call(
        _in_kernel,
        grid=(seq // steps,),
        in_specs=[tok_of(d), _const_spec((1, d)), _const_spec(w_qkv.shape), _const_spec(w_lru.shape),
                  _const_spec(conv_w.shape), _const_spec((1, width)), _const_spec(w_gates.shape),
                  _const_spec((1, width)), _const_spec((1, width)), _const_spec((1, width)),
                  _const_spec((tail_rows, width)), _const_spec((batch, width))],
        out_specs=[tok_of(d), head_spec, head_spec, head_spec, head_spec, head_spec, tok_of(width),
                   pl.BlockSpec((tail_rows, width), lambda i: (0, 0)),
                   pl.BlockSpec((batch, width), lambda i: (0, 0))],
        out_shape=[jax.ShapeDtypeStruct((batch, seq, d), BF16), head_bf, head_bf, head_bf, head_f32, head_f32,
                   jax.ShapeDtypeStruct((batch, seq, width), BF16),
                   jax.ShapeDtypeStruct((tail_rows, width), F32),
                   jax.ShapeDtypeStruct((batch, width), F32)],
        scratch_shapes=[pltpu.VMEM(slab, F32), pltpu.VMEM(slab, F32), pltpu.VMEM((tail_rows, width), F32),
                        pltpu.VMEM((batch, width), F32)],
        compiler_params=_params(("arbitrary",)),
        name="in_proj",
    )(x, g, w_qkv, w_lru, conv_w, conv_b, w_gates, ba, bx, a_logit, conv_state_tm, h0)


def _memkv_kernel(mem_ref, g_ref, wk_ref, wv_ref, kg_ref, k_ref, v_ref):
    nh, n, hd = k_ref.shape
    m = _rms(mem_ref[...], g_ref[...]).astype(BF16)
    k = _dot(m, wk_ref[...])
    v = _dot(m, wv_ref[...])
    for h in range(nh):
        k_ref[h] = _rms(k[:, h * hd:(h + 1) * hd], kg_ref[...])
        v_ref[h] = v[:, h * hd:(h + 1) * hd]


def _memory_kv(mem, g, wk, wv, kg):
    b, n, d = mem.shape
    nh = MEM_HEADS
    hd = wk.shape[1] // nh
    out = jax.ShapeDtypeStruct((b, nh, n, hd), F32)
    out_spec = pl.BlockSpec((None, nh, n, hd), lambda i: (i, 0, 0, 0))
    return pl.pallas_call(
        _memkv_kernel,
        grid=(b,),
        in_specs=[pl.BlockSpec((None, n, d), lambda i: (i, 0, 0)), _const_spec((1, d)), _const_spec(wk.shape),
                  _const_spec(wv.shape), _const_spec((1, hd))],
        out_specs=[out_spec, out_spec],
        out_shape=[out, out],
        compiler_params=_params(("parallel",)),
        name="memory_kv",
    )(mem, g, wk, wv, kg)


def _memory_attention(qm, mk_ref, mv_ref, j, qg):
    _, nh, _, hd = mk_ref.shape
    heads = []
    for h in range(nh):
        qh = _rms(qm[:, h * hd:(h + 1) * hd], qg).astype(BF16)
        s = _dot_nt(qh, mk_ref[j, h].astype(BF16)) / math.sqrt(hd)
        e = jnp.exp(s - jnp.max(s, axis=-1, keepdims=True))
        p = e / jnp.sum(e, axis=-1, keepdims=True)
        heads.append(_dot(p.astype(BF16), mv_ref[j, h].astype(BF16)))
    return jnp.concatenate(heads, axis=1)


def _merge_kernel(x_ref, xn_ref, osb_ref, olru_ref, mk_ref, mv_ref, wqm_ref, qg_ref, wg_ref, bm_ref,
                  wsb_ref, wlru_ref, wmem_ref, wout_ref, o_ref):
    nb = mk_ref.shape[0]
    tm, d = o_ref.shape
    per_seq = tm // nb
    xn = xn_ref[...]

    qm = _dot(xn, wqm_ref[...])
    omem = jnp.concatenate([_memory_attention(qm[j * per_seq:(j + 1) * per_seq], mk_ref, mv_ref, j, qg_ref[...])
                            for j in range(nb)], axis=0).astype(BF16)

    gates = _sigmoid(_dot(xn, wg_ref[...]) + bm_ref[...])
    merged = (gates[:, :d] * _dot(osb_ref[...], wsb_ref[...])
              + gates[:, d:2 * d] * _dot(olru_ref[...], wlru_ref[...])
              + gates[:, 2 * d:] * _dot(omem, wmem_ref[...]))
    o_ref[...] = x_ref[...] + _dot(merged.astype(BF16), wout_ref[...])


def _attn_merge_kernel(x_ref, xn_ref, olru_ref, q_ref, *refs, tiles_per_seq):
    nh, q_rows, hd = q_ref.shape
    nt = (len(refs) - 19) // 2
    k_tiles, v_tiles = refs[:nt + 1], refs[nt + 1:2 * nt + 2]
    (k_hbm, v_hbm, u_ref, mk_ref, mv_ref, wqm_ref, qg_ref, wg_ref, bm_ref, wsb_ref, wlru_ref, wmem_ref, wout_ref,
     o_ref, kbuf, vbuf, sem) = refs[2 * nt + 2:]
    tk = u_ref.shape[1]
    tq = tk
    assert nt == q_rows // tq
    d = o_ref.shape[1]
    seq_index = pl.program_id(0) // tiles_per_seq
    first = (pl.program_id(0) % tiles_per_seq) * nt
    u = u_ref[...]
    mask = _causal_mask(tq, tk)
    units = [(h, t) for t in range(nt) for h in range(nh)]
    qs = [q_ref[h, t * tq:(t + 1) * tq, :] for h, t in units]
    k2s = [jnp.concatenate([k_tiles[t][h], k_tiles[t + 1][h]], axis=0) for h, t in units]
    v2s = [jnp.concatenate([v_tiles[t][h], v_tiles[t + 1][h]], axis=0) for h, t in units]
    has_previous = first > 0
    prev_valid = [has_previous if t == 0 else None for _, t in units]

    xn = xn_ref[...]
    bm = bm_ref[...]
    early = {}

    def gate(n):
        return _sigmoid(_dot(xn, wg_ref[:, n * d:(n + 1) * d]) + bm[:, n * d:(n + 1) * d])

    def project_memory_queries():
        early["qm"] = _dot(xn, wqm_ref[...])

    def attend_memory():
        early["omem"] = _memory_attention(early.pop("qm"), mk_ref, mv_ref, 0, qg_ref[...]).astype(BF16)

    def recurrent_gate():
        early["g1"] = gate(1)

    def recurrent_term():
        early["rest"] = early.pop("g1") * _dot(olru_ref[...], wlru_ref[...])

    def memory_gate():
        early["g2"] = gate(2)

    def memory_term():
        early["rest"] = early["rest"] + early.pop("g2") * _dot(early.pop("omem"), wmem_ref[...])

    def sweep_gate():
        early["g0"] = gate(0)

    gaps = [project_memory_queries, attend_memory, recurrent_gate, recurrent_term, memory_gate, memory_term,
            sweep_gate]
    gaps += [None] * (2 * len(units) // SB_HEADS_PER_STEP - len(gaps))
    accs, cs = (), ()
    for start in range(0, len(units), SB_HEADS_PER_STEP):
        sel = slice(start, start + SB_HEADS_PER_STEP)
        acc_g, c_g = _sb_own_and_previous(qs[sel], k2s[sel], v2s[sel], u, mask, prev_valid[sel],
                                          (gaps.pop(0), gaps.pop(0)))
        accs, cs = accs + acc_g, cs + c_g
    assert not gaps

    def sweep_rest():
        out = []
        for t in range(nt):
            sel = slice(t * nh, (t + 1) * nh)

            def tiles_fn(kb, cs_t, sel=sel):
                _fetch_key_tile(k_hbm, v_hbm, seq_index, kb, kbuf, vbuf, sem)
                return _sb_tiles(qs[sel], [kbuf[h] for h in range(nh)], [vbuf[h] for h in range(nh)], u, cs_t,
                                 None)

            out.extend(_sb_sweep_rest(first + t - 2, tiles_fn, accs[sel], cs[sel]))
        return tuple(out)

    accs = lax.cond(_min_all(cs) < SB_EXIT_DROP, sweep_rest, lambda: accs)
    osb = jnp.concatenate([jnp.concatenate(accs[t * nh:(t + 1) * nh], axis=1) for t in range(nt)],
                          axis=0).astype(BF16)

    merged = early["g0"] * _dot(osb, wsb_ref[...]) + early["rest"]
    o_ref[...] = x_ref[...] + _dot(merged.astype(BF16), wout_ref[...])


def _attn_merge(x2, xn, olru, q, k, v, u, mk, mv, wqm, qg, wg, bm, wsb, wlru, wmem, wout):
    m, d = x2.shape
    batch, nh, seq, hd = q.shape
    _, nmh, n_mem, mhd = mk.shape
    tk = u.shape[1]
    nt = SB_QUERY_TILES_PER_STEP
    tm = nt * tk
    tiles_per_seq = seq // tm
    assert seq % tm == 0
    tok_of = lambda width: pl.BlockSpec((tm, width), lambda i: (i, 0))
    q_spec = pl.BlockSpec((None, nh, tm, hd), lambda i: (i // tiles_per_seq, 0, i % tiles_per_seq, 0))

    def key_tile(offset):
        return pl.BlockSpec((None, nh, tk, hd), lambda i: (i // tiles_per_seq, 0,
                                                           jnp.maximum((i % tiles_per_seq) * nt + offset, 0), 0))

    any_spec = pl.BlockSpec(memory_space=pl.ANY)
    mem_spec = pl.BlockSpec((1, nmh, n_mem, mhd), lambda i: (i // tiles_per_seq, 0, 0, 0))
    key_tiles = [key_tile(offset) for offset in range(-1, nt)]
    return pl.pallas_call(
        functools.partial(_attn_merge_kernel, tiles_per_seq=tiles_per_seq),
        grid=(m // tm,),
        in_specs=[tok_of(d), tok_of(d), tok_of(olru.shape[1]), q_spec, *key_tiles, *key_tiles, any_spec,
                  any_spec, _const_spec(u.shape), mem_spec,
                  mem_spec, _const_spec(wqm.shape), _const_spec((1, mhd)), _const_spec(wg.shape),
                  _const_spec((1, N_BRANCH * d)), _const_spec(wsb.shape), _const_spec(wlru.shape),
                  _const_spec(wmem.shape), _const_spec(wout.shape)],
        out_specs=tok_of(d),
        out_shape=jax.ShapeDtypeStruct((m, d), F32),
        scratch_shapes=[pltpu.VMEM((nh, tk, hd), BF16), pltpu.VMEM((nh, tk, hd), BF16),
                        pltpu.SemaphoreType.DMA((2,))],
        compiler_params=_params(("parallel",)),
        name="attn_merge",
    )(x2, xn, olru, q, *([k] * (nt + 1)), *([v] * (nt + 1)), k, v, u, mk, mv, wqm, qg, wg, bm, wsb, wlru, wmem,
      wout)


def _merge(x2, xn, osb, olru, mk, mv, wqm, qg, wg, bm, wsb, wlru, wmem, wout, seq):
    m, d = x2.shape
    _, nh, n_mem, hd = mk.shape
    if seq >= TOKEN_TILE:
        tm, nb = TOKEN_TILE, 1
        per_batch = seq // tm
        mem_spec = pl.BlockSpec((nb, nh, n_mem, hd), lambda i: (i // per_batch, 0, 0, 0))
    else:
        nb = MERGE_SHORT_SEQUENCES
        tm = nb * seq
        mem_spec = pl.BlockSpec((nb, nh, n_mem, hd), lambda i: (i, 0, 0, 0))
    tok_of = lambda width: pl.BlockSpec((tm, width), lambda i: (i, 0))
    return pl.pallas_call(
        _merge_kernel,
        grid=(m // tm,),
        in_specs=[tok_of(d), tok_of(d), tok_of(osb.shape[1]), tok_of(olru.shape[1]), mem_spec, mem_spec,
                  _const_spec(wqm.shape), _const_spec((1, hd)), _const_spec(wg.shape),
                  _const_spec((1, N_BRANCH * d)), _const_spec(wsb.shape), _const_spec(wlru.shape),
                  _const_spec(wmem.shape), _const_spec(wout.shape)],
        out_specs=tok_of(d),
        out_shape=jax.ShapeDtypeStruct((m, d), F32),
        compiler_params=_params(("parallel",)),
        name="merge",
    )(x2, xn, osb, olru, mk, mv, wqm, qg, wg, bm, wsb, wlru, wmem, wout)


def _ffn_kernel(x_ref, g_ref, wgate_ref, wup_ref, wdown_ref, o_ref):
    for start in range(0, x_ref.shape[0], TOKEN_TILE):
        rows = slice(start, start + TOKEN_TILE)
        x = x_ref[rows, :]
        xn = _rms(x, g_ref[...]).astype(BF16)
        gate = _dot(xn, wgate_ref[...])
        up = _dot(xn, wup_ref[...])
        hidden = (gate * _sigmoid(gate) * up).astype(BF16)
        o_ref[rows, :] = x + _dot(hidden, wdown_ref[...])


def _ffn(x1, g, wgate, wup, wdown):
    m, d = x1.shape
    tm = FFN_TILES_PER_STEP * TOKEN_TILE if m % (FFN_TILES_PER_STEP * TOKEN_TILE) == 0 else TOKEN_TILE
    assert m % tm == 0
    tok = pl.BlockSpec((tm, d), lambda i: (i, 0))
    return pl.pallas_call(
        _ffn_kernel,
        grid=(m // tm,),
        in_specs=[tok, _const_spec((1, d)), _const_spec(wgate.shape), _const_spec(wup.shape),
                  _const_spec(wdown.shape)],
        out_specs=tok,
        out_shape=jax.ShapeDtypeStruct((m, d), F32),
        compiler_params=_params(("parallel",)),
        name="ffn",
    )(x1, g, wgate, wup, wdown)


def _layer(x, past_k, past_v, conv_state, h0, mem_k, mem_v, lw):
    b, t, d = x.shape
    x2 = x.reshape(b * t, d)
    sbw = SB_HEADS * SB_HEAD_DIM
    lru_w = lw["conv_w"].shape[1]
    tail = CONV_WIDTH - 1
    row = lambda v: v.reshape(1, -1)

    lru0 = 3 * sbw
    w_gates = jnp.concatenate([lw["lru_wa"], lw["lru_wx"]], axis=2)
    xn, q, k_bf, v_bf, k_new, v_new, o_lru, new_conv_tm, h_last = _in_proj(
        x, row(lw["norm_mix_g"]), lw["w_in"][:, :lru0], lw["w_in"][:, lru0:lru0 + 2 * lru_w], lw["conv_w"],
        row(lw["conv_b"]), w_gates, row(lw["lru_ba"]), row(lw["lru_bx"]), row(lw["lru_a_logit"]),
        conv_state.transpose(1, 0, 2).reshape(tail * b, lru_w), h0)
    new_conv = new_conv_tm.reshape(tail, b, lru_w).transpose(1, 0, 2)

    qm0 = lru0 + 2 * lru_w
    mem_w = lw["w_br_mem"].shape[0]
    merge_weights = (lw["w_in"][:, qm0:qm0 + mem_w], row(lw["q_norm_g"]), lw["w_in"][:, qm0 + mem_w:],
                     row(lw["b_merge"]), lw["w_br_sb"], lw["w_br_lru"], lw["w_br_mem"], lw["w_out"])
    xn2, o_lru2 = xn.reshape(b * t, d), o_lru.reshape(b * t, lru_w)
    if past_k is None:
        x1 = _attn_merge(x2, xn2, o_lru2, q, k_bf, v_bf, _suffix_sum_matrix(SB_TILE), mem_k.astype(BF16),
                         mem_v.astype(BF16), *merge_weights)
    else:
        o_sb = _sb_sample(q, k_bf, v_bf, past_k, past_v, _suffix_sum_matrix(SB_TILE), _suffix_sum_matrix(t))
        x1 = _merge(x2, xn2, o_sb, o_lru2, mem_k, mem_v, *merge_weights, t)
    y = _ffn(x1, row(lw["norm_ffn_g"]), lw["w_ffn_gate"], lw["w_ffn_up"], lw["w_ffn_down"])
    return y.reshape(b, t, d), k_new, v_new, new_conv, h_last


_BF16_WEIGHTS = ("w_in", "lru_wa", "lru_wx", "w_br_sb", "w_br_lru", "w_br_mem", "w_out", "w_ffn_gate",
                 "w_ffn_up", "w_ffn_down")


def kernel(x_prompt, x_sample, mem_prompt, cache_sb_k, cache_sb_v, state_conv, state_lru_h, cache_mem_k, cache_mem_v, norm_mix_g, w_in, b_merge, conv_w, conv_b, lru_wa, lru_ba, lru_wx, lru_bx, lru_a_logit, q_norm_g, k_norm_g, mem_norm_g, w_mem_k, w_mem_v, w_br_sb, w_br_lru, w_br_mem, w_out, norm_ffn_g, w_ffn_gate, w_ffn_up, w_ffn_down):
    depth = w_in.shape[0]
    bp = x_prompt.shape[0]
    lru_w = conv_w.shape[2]
    hp, hs = x_prompt, x_sample
    outs = [[] for _ in range(10)]
    for l in range(depth):
        lw = {
            "norm_mix_g": norm_mix_g[l], "w_in": w_in[l], "b_merge": b_merge[l], "conv_w": conv_w[l],
            "conv_b": conv_b[l], "lru_wa": lru_wa[l], "lru_ba": lru_ba[l], "lru_wx": lru_wx[l],
            "lru_bx": lru_bx[l], "lru_a_logit": lru_a_logit[l], "q_norm_g": q_norm_g[l],
            "w_br_sb": w_br_sb[l], "w_br_lru": w_br_lru[l], "w_br_mem": w_br_mem[l], "w_out": w_out[l],
            "norm_ffn_g": norm_ffn_g[l], "w_ffn_gate": w_ffn_gate[l], "w_ffn_up": w_ffn_up[l],
            "w_ffn_down": w_ffn_down[l],
        }
        for name in _BF16_WEIGHTS:
            lw[name] = lw[name].astype(BF16)
        mk_p, mv_p = _memory_kv(mem_prompt, mem_norm_g[l].reshape(1, -1), w_mem_k[l].astype(BF16),
                                w_mem_v[l].astype(BF16), k_norm_g[l].reshape(1, -1))
        hp, k_p, v_p, c_p, s_p = _layer(hp, None, None, jnp.zeros((bp, CONV_WIDTH - 1, lru_w), F32),
                                        jnp.zeros((bp, lru_w), F32), mk_p, mv_p, lw)
        hs, k_s, v_s, c_s, s_s = _layer(hs, cache_sb_k[l], cache_sb_v[l], state_conv[l], state_lru_h[l],
                                        cache_mem_k[l], cache_mem_v[l], lw)
        for lst, val in zip(outs, (k_p, v_p, c_p, s_p, mk_p, mv_p, k_s, v_s, c_s, s_s)):
            lst.append(val)
    return (hp, hs) + tuple(lst[0][None] if depth == 1 else jnp.stack(lst) for lst in outs)
```
